```python
import jax
import jax.numpy as jnp
from jax import lax
import numpy as np

D_MODEL = 1024
BATCH = 16
SEQ = 2048
DEPTH = 1

D_MIX = D_MODEL
ATT_HEADS = 8
ATT_KV_HEADS = 2
ATT_HEAD_DIM = D_MIX // 16
ATT_WIDTH = ATT_HEADS * ATT_HEAD_DIM
ATT_KV_WIDTH = ATT_KV_HEADS * ATT_HEAD_DIM
WINDOW = 128
MLSTM_HEADS = 4
MLSTM_WIDTH = D_MIX - ATT_WIDTH
MLSTM_HEAD_DIM = MLSTM_WIDTH // MLSTM_HEADS
MLSTM_CHUNK = 64
QK_CONV_WIDTH = 4
IN_COLS = ATT_WIDTH + 2 * ATT_KV_WIDTH + 4 * MLSTM_WIDTH + 2 * MLSTM_HEADS
X_HEADS = 4
X_HEAD_DIM = D_MODEL // X_HEADS
MEM_LEN = 256
D_FF = 2816
FFN_CONV_WIDTH = 3
RMS_EPS = 1e-6
NEG_BIG = -1e30

kernel_name = 'hybrid_swa_mlstm_convffn_block'


def rmsnorm(x, g):
    xf = x.astype(jnp.float32)
    y = xf * lax.rsqrt(jnp.mean(xf * xf, axis=-1, keepdims=True) + RMS_EPS)
    return (y * g.astype(jnp.float32)).astype(x.dtype)


def causal_depthwise_conv(x, w, b):
    K, C = w.shape
    y = lax.conv_general_dilated(
        x, w.astype(x.dtype)[:, None, :], window_strides=(1,),
        padding=[(K - 1, 0)], dimension_numbers=('NWC', 'WIO', 'NWC'),
        feature_group_count=C)
    return y + b.astype(x.dtype)


def alibi_slopes(n_heads):
    return 2.0 ** (-8.0 * jnp.arange(1, n_heads + 1, dtype=jnp.float32) / n_heads)


def sliding_window_attention(q, k, v, sinks, slopes):
    B, S, _, D = q.shape
    W = WINDOW
    nb = S // W
    G = ATT_HEADS // ATT_KV_HEADS
    qb = q.reshape(B, nb, W, ATT_KV_HEADS, G, D)
    kb = k.reshape(B, nb, W, ATT_KV_HEADS, D)
    vb = v.reshape(B, nb, W, ATT_KV_HEADS, D)
    pad = ((0, 0), (1, 0), (0, 0), (0, 0), (0, 0))
    kk = jnp.concatenate([jnp.pad(kb, pad)[:, :-1], kb], axis=2)
    vv = jnp.concatenate([jnp.pad(vb, pad)[:, :-1], vb], axis=2)
    s = jnp.einsum('bnqhgd,bnkhd->bnhgqk', qb, kk,
                   preferred_element_type=jnp.float32) * (D ** -0.5)
    qi = jnp.arange(W)[:, None]
    kj = jnp.arange(2 * W)[None, :]
    dist = qi + W - kj
    in_band = (dist >= 0) & (dist < W)
    valid = in_band[None] & ((jnp.arange(nb)[:, None, None] > 0) | (kj >= W)[None])
    bias = -slopes.reshape(ATT_KV_HEADS, G)[:, :, None, None] * dist.astype(jnp.float32)
    s = jnp.where(valid[None, :, None, None], s + bias[None, None], NEG_BIG)
    sink = sinks.astype(jnp.float32).reshape(ATT_KV_HEADS, G)[None, None, :, :, None, None]
    m = jnp.maximum(jnp.max(s, axis=-1, keepdims=True), sink)
    p = jnp.exp(s - m)
    p = p / (jnp.sum(p, axis=-1, keepdims=True) + jnp.exp(sink - m))
    o = jnp.einsum('bnhgqk,bnkhd->bnqhgd', p.astype(vv.dtype), vv)
    return o.reshape(B, S, ATT_HEADS * D)


def mlstm_chunkwise(q, k, v, i_pre, f_pre):
    B, S, H, D = q.shape
    L = MLSTM_CHUNK
    nc = S // L
    q = q * (D ** -0.5)

    def to_chunks(a):
        return a.reshape(B, nc, L, H, D).transpose(1, 0, 3, 2, 4)

    def gate_chunks(a):
        return a.reshape(B, nc, L, H).transpose(1, 0, 3, 2)

    causal = jnp.tril(jnp.ones((L, L), dtype=bool))

    def step(carry, inp):
        C, n, m = carry
        qc, kc, vc, igc, lfc = inp
        b = jnp.cumsum(lfc, axis=-1)
        dmat = jnp.where(causal, b[..., :, None] - b[..., None, :] + igc[..., None, :], NEG_BIG)
        inter = b + m[..., None]
        m_t = jnp.maximum(inter, jnp.max(dmat, axis=-1))
        w_inter = jnp.exp(inter - m_t)
        sc = jnp.einsum('bhtd,bhsd->bhts', qc, kc) * jnp.exp(dmat - m_t[..., None])
        num = (w_inter[..., None] * jnp.einsum('bhtd,bhde->bhte', qc, C)
               + jnp.einsum('bhts,bhse->bhte', sc, vc))
        den = w_inter * jnp.einsum('bhtd,bhd->bht', qc, n) + jnp.sum(sc, axis=-1)
        h = num / jnp.maximum(jnp.abs(den), jnp.exp(-m_t))[..., None]
        b_end = b[..., -1]
        g = b_end[..., None] - b + igc
        m_new = jnp.maximum(b_end + m, jnp.max(g, axis=-1))
        decay = jnp.exp(b_end + m - m_new)
        ws = jnp.exp(g - m_new[..., None])
        C_new = decay[..., None, None] * C + jnp.einsum('bhs,bhsd,bhse->bhde', ws, kc, vc)
        n_new = decay[..., None] * n + jnp.einsum('bhs,bhsd->bhd', ws, kc)
        return (C_new, n_new, m_new), h

    init = (jnp.zeros((B, H, D, D), jnp.float32),
            jnp.zeros((B, H, D), jnp.float32),
            jnp.zeros((B, H), jnp.float32))
    xs = (to_chunks(q), to_chunks(k), to_chunks(v),
          gate_chunks(i_pre), gate_chunks(jax.nn.log_sigmoid(f_pre)))
    _, h = lax.scan(step, init, xs)
    return h.transpose(1, 0, 3, 2, 4).reshape(B, S, H * D)


def cross_attention(hq, hm, wq, wkv, wo):
    B, S, _ = hq.shape
    M = hm.shape[1]
    q = (hq @ wq).reshape(B, S, X_HEADS, X_HEAD_DIM)
    kv = hm @ wkv
    k = kv[..., :D_MODEL].reshape(B, M, X_HEADS, X_HEAD_DIM)
    v = kv[..., D_MODEL:].reshape(B, M, X_HEADS, X_HEAD_DIM)
    s = jnp.einsum('bshd,bmhd->bhsm', q, k,
                   preferred_element_type=jnp.float32) * (X_HEAD_DIM ** -0.5)
    a = jax.nn.softmax(s, axis=-1).astype(v.dtype)
    o = jnp.einsum('bhsm,bmhd->bshd', a, v).reshape(B, S, X_HEADS * X_HEAD_DIM)
    return o @ wo


def setup_inputs(seed: int = 0) -> dict:
    key = jax.random.key(seed)
    ks = jax.random.split(key, 22)
    f32 = jnp.float32

    def nrm(k, shape, scale):
        return jax.random.normal(k, shape, f32) * scale

    def gain(k, shape):
        return 1.0 + 0.02 * jax.random.normal(k, shape, f32)

    L = DEPTH
    f_bias = jnp.linspace(3.0, 6.0, MLSTM_HEADS, dtype=f32)
    return {
        'x': nrm(ks[0], (BATCH, SEQ, D_MODEL), 1.0),
        'mem': nrm(ks[1], (BATCH, MEM_LEN, D_MODEL), 1.0),
        'norm_mix_g': gain(ks[2], (L, D_MODEL)),
        'w_in': nrm(ks[3], (L, D_MODEL, IN_COLS), D_MODEL ** -0.5),
        'b_gate_if': jnp.concatenate(
            [nrm(ks[4], (L, MLSTM_HEADS), 0.1),
             f_bias[None] + nrm(ks[5], (L, MLSTM_HEADS), 0.1)], axis=-1),
        'conv_qk_w': nrm(ks[6], (L, QK_CONV_WIDTH, 2 * MLSTM_WIDTH), QK_CONV_WIDTH ** -0.5),
        'conv_qk_b': nrm(ks[7], (L, 2 * MLSTM_WIDTH), 0.02),
        'attn_sinks': nrm(ks[8], (L, ATT_HEADS), 0.5),
        'w_out': nrm(ks[9], (L, D_MIX, D_MODEL), D_MIX ** -0.5),
        'norm_xattn_g': gain(ks[10], (L, D_MODEL)),
        'norm_mem_g': gain(ks[11], (L, D_MODEL)),
        'wq_x': nrm(ks[12], (L, D_MODEL, X_HEADS * X_HEAD_DIM), D_MODEL ** -0.5),
        'wkv_x': nrm(ks[13], (L, D_MODEL, 2 * X_HEADS * X_HEAD_DIM), D_MODEL ** -0.5),
        'wo_x': nrm(ks[14], (L, X_HEADS * X_HEAD_DIM, D_MODEL), D_MODEL ** -0.5),
        'norm_ffn_g': gain(ks[15], (L, D_MODEL)),
        'w_up': nrm(ks[16], (L, D_MODEL, 2 * D_FF), D_MODEL ** -0.5),
        'conv_ffn_w': nrm(ks[17], (L, FFN_CONV_WIDTH, D_FF), FFN_CONV_WIDTH ** -0.5),
        'conv_ffn_b': nrm(ks[18], (L, D_FF), 0.02),
        'w_down': nrm(ks[19], (L, D_FF, D_MODEL), D_FF ** -0.5),
        'norm_final_g': gain(ks[20], (D_MODEL,)),
    }


def reference(x, mem, norm_mix_g, w_in, b_gate_if, conv_qk_w, conv_qk_b, attn_sinks,
              w_out, norm_xattn_g, norm_mem_g, wq_x, wkv_x, wo_x, norm_ffn_g, w_up,
              conv_ffn_w, conv_ffn_b, w_down, norm_final_g):
    B, S, _ = x.shape
    H = MLSTM_HEADS
    slopes = alibi_slopes(ATT_HEADS)
    o0 = ATT_WIDTH
    o1 = o0 + ATT_KV_WIDTH
    o2 = o1 + ATT_KV_WIDTH
    o3 = o2 + 2 * MLSTM_WIDTH
    o4 = o3 + MLSTM_WIDTH
    o5 = o4 + MLSTM_WIDTH
    for l in range(DEPTH):
        h = rmsnorm(x, norm_mix_g[l])
        p = h @ w_in[l]
        aq = p[..., :o0].reshape(B, S, ATT_HEADS, ATT_HEAD_DIM)
        ak = p[..., o0:o1].reshape(B, S, ATT_KV_HEADS, ATT_HEAD_DIM)
        av = p[..., o1:o2].reshape(B, S, ATT_KV_HEADS, ATT_HEAD_DIM)
        attn_out = sliding_window_attention(aq, ak, av, attn_sinks[l], slopes)

        mqk = jax.nn.silu(causal_depthwise_conv(p[..., o2:o3], conv_qk_w[l], conv_qk_b[l]))
        mq = mqk[..., :MLSTM_WIDTH].reshape(B, S, H, MLSTM_HEAD_DIM).astype(jnp.float32)
        mk = mqk[..., MLSTM_WIDTH:].reshape(B, S, H, MLSTM_HEAD_DIM).astype(jnp.float32)
        mv = p[..., o3:o4].reshape(B, S, H, MLSTM_HEAD_DIM).astype(jnp.float32)
        mo = p[..., o4:o5].astype(jnp.float32)
        gif = p[..., o5:].astype(jnp.float32) + b_gate_if[l].astype(jnp.float32)
        m_h = mlstm_chunkwise(mq, mk, mv, gif[..., :H], gif[..., H:])
        mlstm_out = (jax.nn.sigmoid(mo) * m_h).astype(x.dtype)

        x = x + jnp.concatenate([attn_out, mlstm_out], axis=-1) @ w_out[l]

        x = x + cross_attention(rmsnorm(x, norm_xattn_g[l]), rmsnorm(mem, norm_mem_g[l]),
                                wq_x[l], wkv_x[l], wo_x[l])

        hf = rmsnorm(x, norm_ffn_g[l])
        u = hf @ w_up[l]
        g = causal_depthwise_conv(u[..., :D_FF], conv_ffn_w[l], conv_ffn_b[l])
        x = x + (jax.nn.silu(g) * u[..., D_FF:]) @ w_down[l]
    return rmsnorm(x, norm_final_g)
```

```python
import functools

import jax
import jax.numpy as jnp
from jax import lax
from jax.experimental import pallas as pl
from jax.experimental.pallas import tpu as pltpu

D_MODEL = 1024
ATT_HEADS = 8
ATT_KV_HEADS = 2
ATT_GROUP = ATT_HEADS // ATT_KV_HEADS
ATT_HEAD_DIM = 64
ATT_WIDTH = ATT_HEADS * ATT_HEAD_DIM
ATT_KV_WIDTH = ATT_KV_HEADS * ATT_HEAD_DIM
WINDOW = 128
MLSTM_HEADS = 4
MLSTM_WIDTH = 512
MLSTM_HEAD_DIM = 128
QK_CONV_WIDTH = 4
X_HEADS = 4
X_HEAD_DIM = 256
MEM_LEN = 256
D_FF = 2816
FFN_CONV_WIDTH = 3
RMS_EPS = 1e-6
NEG_BIG = -1e30

O_AQ = 0
O_AK = ATT_WIDTH
O_AV = O_AK + ATT_KV_WIDTH
O_MQK = O_AV + ATT_KV_WIDTH
O_MV = O_MQK + 2 * MLSTM_WIDTH
O_MO = O_MV + MLSTM_WIDTH
IN_MAIN = O_MO + MLSTM_WIDTH
GATE_PAD = 128

SEQ_TILE = 256
MLSTM_CHUNK = 128
FFN_COLS = 256
CARRY_ROWS = 8
VMEM_LIMIT = 56 * 1024 * 1024


def _rmsnorm(x, g):
    return x * lax.rsqrt(jnp.mean(x * x, axis=-1, keepdims=True) + RMS_EPS) * g


def _dot(a, b):
    return jnp.dot(a, b, preferred_element_type=jnp.float32)


def _dot_nt(a, b):
    return lax.dot_general(a, b, (((1,), (1,)), ((), ())),
                           preferred_element_type=jnp.float32)


def _log_sigmoid(x):
    return jnp.minimum(x, 0.0) - jnp.log1p(jnp.exp(-jnp.abs(x)))


def _sigmoid(x):
    return 1.0 / (1.0 + jnp.exp(-x))


def _memkv_kernel(mem_ref, g_ref, wkv_ref, k_ref, v_ref):
    hm = _rmsnorm(mem_ref[0], g_ref[...]).astype(jnp.bfloat16)
    kv = _dot(hm, wkv_ref[...])
    k_ref[0] = kv[:, :D_MODEL].astype(jnp.bfloat16)
    v_ref[0] = kv[:, D_MODEL:].astype(jnp.bfloat16)


def _mixer_kernel(sinks_ref, x_ref, g_ref, win_ref, wgate_ref, bgate_ref, convw_ref,
                  convb_ref, wout_ref, o_ref,
                  kv_buf_ref, qk_buf_ref, c_ref, m_ref, cat_ref, *, slopes):
    s_idx = pl.program_id(1)
    ts = SEQ_TILE
    L = MLSTM_CHUNK

    @pl.when(s_idx == 0)
    def _():
        kv_buf_ref[0:WINDOW, :] = jnp.zeros((WINDOW, 2 * ATT_KV_WIDTH), jnp.bfloat16)
        qk_buf_ref[0:CARRY_ROWS, :] = jnp.zeros((CARRY_ROWS, 2 * MLSTM_WIDTH), jnp.float32)
        c_ref[...] = jnp.zeros_like(c_ref)
        m_ref[...] = jnp.zeros_like(m_ref)

    @pl.when(s_idx > 0)
    def _():
        kv_buf_ref[0:WINDOW, :] = kv_buf_ref[ts:ts + WINDOW, :]
        qk_buf_ref[0:CARRY_ROWS, :] = qk_buf_ref[ts:ts + CARRY_ROWS, :]

    x = x_ref[0]
    h = _rmsnorm(x, g_ref[...]).astype(jnp.bfloat16)
    p = _dot(h, win_ref[...])
    gates = _dot(h, wgate_ref[...]) + bgate_ref[...]

    W = WINDOW
    qi = lax.broadcasted_iota(jnp.int32, (W, 2 * W), 0)
    kj = lax.broadcasted_iota(jnp.int32, (W, 2 * W), 1)
    dist = qi + W - kj
    in_band = (dist >= 0) & (dist < W)
    first_thr = jnp.where(s_idx > 0, 0, W)
    distf = dist.astype(jnp.float32)

    q_all = (p[:, O_AQ:O_AQ + ATT_WIDTH] * (ATT_HEAD_DIM ** -0.5)).astype(jnp.bfloat16)
    kv_buf_ref[W:W + ts, :] = p[:, O_AK:O_AK + 2 * ATT_KV_WIDTH].astype(jnp.bfloat16)
    for blk in range(ts // W):
        r0 = blk * W
        both = kv_buf_ref[r0:r0 + 2 * W, :]
        valid = (in_band & (kj >= first_thr)) if blk == 0 else in_band
        for hk in range(ATT_KV_HEADS):
            kk = both[:, hk * ATT_HEAD_DIM:(hk + 1) * ATT_HEAD_DIM]
            vv = both[:, ATT_KV_WIDTH + hk * ATT_HEAD_DIM:ATT_KV_WIDTH + (hk + 1) * ATT_HEAD_DIM]
            for gq in range(ATT_GROUP):
                head = hk * ATT_GROUP + gq
                qh = q_all[r0:r0 + W, head * ATT_HEAD_DIM:(head + 1) * ATT_HEAD_DIM]
                sc = _dot_nt(qh, kk)
                sc = jnp.where(valid, sc - slopes[head] * distf, NEG_BIG)
                sink = sinks_ref[head]
                mx = jnp.maximum(jnp.max(sc, axis=-1, keepdims=True), sink)
                e = jnp.exp(sc - mx)
                den = jnp.sum(e, axis=-1, keepdims=True) + jnp.exp(sink - mx)
                o = _dot(e.astype(jnp.bfloat16), vv) / den
                cat_ref[r0:r0 + W, head * ATT_HEAD_DIM:(head + 1) * ATT_HEAD_DIM] = (
                    o.astype(jnp.bfloat16))

    qk_buf_ref[CARRY_ROWS:CARRY_ROWS + ts, :] = p[:, O_MQK:O_MQK + 2 * MLSTM_WIDTH]
    conv = convb_ref[...]
    for j in range(QK_CONV_WIDTH):
        shift = QK_CONV_WIDTH - 1 - j
        conv = conv + convw_ref[j:j + 1, :] * qk_buf_ref[CARRY_ROWS - shift:CARRY_ROWS - shift + ts, :]
    mqk = conv * _sigmoid(conv)

    row = lax.broadcasted_iota(jnp.int32, (L, L), 0)
    colL = lax.broadcasted_iota(jnp.int32, (L, L), 1)
    causal = row >= colL
    tril = causal.astype(jnp.float32)
    lane = lax.broadcasted_iota(jnp.int32, (L, GATE_PAD), 1)
    ones_col = (lane == 0).astype(jnp.bfloat16)
    for c in range(ts // L):
        r0 = c * L
        gc = gates[r0:r0 + L]
        lf = jnp.where((lane >= MLSTM_HEADS) & (lane < 2 * MLSTM_HEADS), _log_sigmoid(gc), 0.0)
        cum = jnp.dot(tril, lf, precision=lax.Precision.HIGHEST,
                      preferred_element_type=jnp.float32)
        bm = jnp.where(lane < MLSTM_HEADS, gc, cum)
        bt = bm.T
        for hd in range(MLSTM_HEADS):
            c0 = hd * MLSTM_HEAD_DIM
            b_col = bm[:, MLSTM_HEADS + hd:MLSTM_HEADS + hd + 1]
            ig_col = bm[:, hd:hd + 1]
            b_row = bt[MLSTM_HEADS + hd:MLSTM_HEADS + hd + 1, :]
            ig_row = bt[hd:hd + 1, :]
            m_prev = m_ref[hd:hd + 1, 0:1]
            dmat = jnp.where(causal, b_col - b_row + ig_row, NEG_BIG)
            inter = b_col + m_prev
            m_t = jnp.maximum(inter, jnp.max(dmat, axis=-1, keepdims=True))
            w_inter = jnp.exp(inter - m_t)
            dexp = jnp.exp(dmat - m_t)
            qf = mqk[r0:r0 + L, c0:c0 + MLSTM_HEAD_DIM] * (MLSTM_HEAD_DIM ** -0.5)
            kf = mqk[r0:r0 + L, MLSTM_WIDTH + c0:MLSTM_WIDTH + c0 + MLSTM_HEAD_DIM]
            qb = qf.astype(jnp.bfloat16)
            kb = kf.astype(jnp.bfloat16)
            vb = p[r0:r0 + L, O_MV + c0:O_MV + c0 + MLSTM_HEAD_DIM].astype(jnp.bfloat16)
            v_ext = jnp.concatenate([vb, ones_col], axis=1)
            sc = _dot_nt(qb, kb) * dexp
            c_old = c_ref[hd]
            num_ext = (w_inter * _dot(qb, c_old.astype(jnp.bfloat16))
                       + _dot(sc.astype(jnp.bfloat16), v_ext))
            num = num_ext[:, :MLSTM_HEAD_DIM]
            den = num_ext[:, MLSTM_HEAD_DIM:MLSTM_HEAD_DIM + 1]
            hh = num * (1.0 / jnp.maximum(jnp.abs(den), jnp.exp(-m_t)))
            og = _sigmoid(p[r0:r0 + L, O_MO + c0:O_MO + c0 + MLSTM_HEAD_DIM])
            cat_ref[r0:r0 + L, ATT_WIDTH + c0:ATT_WIDTH + c0 + MLSTM_HEAD_DIM] = (
                (og * hh).astype(jnp.bfloat16))
            b_end = b_col[L - 1:L, :]
            g_col = b_end - b_col + ig_col
            m_new = jnp.maximum(b_end + m_prev, jnp.max(g_col, axis=0, keepdims=True))
            decay = jnp.exp(b_end + m_prev - m_new)
            ws = jnp.exp(g_col - m_new)
            kwt = (kf * ws).T.astype(jnp.bfloat16)
            c_ref[hd] = decay * c_old + _dot(kwt, v_ext)
            m_ref[hd:hd + 1, :] = jnp.broadcast_to(m_new, (1, GATE_PAD))

    o_ref[0] = x + _dot(cat_ref[...], wout_ref[...])


def _ffn_kernel(x_ref, k_ref, v_ref, gx_ref, wq_ref, wo_ref, gf_ref, wup_ref, convw_ref,
                convb_ref, wdown_ref, gfin_ref, o_ref, cat_ref, u_buf_ref, act_ref):
    s_idx = pl.program_id(1)
    ts = SEQ_TILE

    @pl.when(s_idx == 0)
    def _():
        u_buf_ref[0:CARRY_ROWS, :] = jnp.zeros((CARRY_ROWS, D_FF), jnp.float32)

    @pl.when(s_idx > 0)
    def _():
        u_buf_ref[0:CARRY_ROWS, :] = u_buf_ref[ts:ts + CARRY_ROWS, :]

    x1 = x_ref[0]
    hq = _rmsnorm(x1, gx_ref[...]).astype(jnp.bfloat16)
    q = (_dot(hq, wq_ref[...]) * (X_HEAD_DIM ** -0.5)).astype(jnp.bfloat16)
    for hd in range(X_HEADS):
        c0 = hd * X_HEAD_DIM
        sc = _dot_nt(q[:, c0:c0 + X_HEAD_DIM], k_ref[0, :, c0:c0 + X_HEAD_DIM])
        e = jnp.exp(sc - jnp.max(sc, axis=-1, keepdims=True))
        den = jnp.sum(e, axis=-1, keepdims=True)
        o = _dot(e.astype(jnp.bfloat16), v_ref[0, :, c0:c0 + X_HEAD_DIM]) / den
        cat_ref[:, c0:c0 + X_HEAD_DIM] = o.astype(jnp.bfloat16)
    x2 = x1 + _dot(cat_ref[...], wo_ref[...])

    hf = _rmsnorm(x2, gf_ref[...]).astype(jnp.bfloat16)
    for j in range(D_FF // FFN_COLS):
        c0 = j * FFN_COLS
        u_buf_ref[CARRY_ROWS:CARRY_ROWS + ts, c0:c0 + FFN_COLS] = _dot(hf, wup_ref[:, c0:c0 + FFN_COLS])
        up = _dot(hf, wup_ref[:, D_FF + c0:D_FF + c0 + FFN_COLS])
        g = convb_ref[:, c0:c0 + FFN_COLS]
        for t in range(FFN_CONV_WIDTH):
            shift = FFN_CONV_WIDTH - 1 - t
            g = g + (convw_ref[t:t + 1, c0:c0 + FFN_COLS]
                     * u_buf_ref[CARRY_ROWS - shift:CARRY_ROWS - shift + ts, c0:c0 + FFN_COLS])
        act_ref[:, c0:c0 + FFN_COLS] = (g * _sigmoid(g) * up).astype(jnp.bfloat16)
    x3 = x2 + _dot(act_ref[...], wdown_ref[...])
    o_ref[0] = _rmsnorm(x3, gfin_ref[...])


def _const_spec(shape):
    return pl.BlockSpec(shape, lambda *_: (0,) * len(shape), pipeline_mode=pl.Buffered(1))


def _row(a):
    return a.reshape(1, -1).astype(jnp.float32)


def _tile_spec():
    return pl.BlockSpec((1, SEQ_TILE, D_MODEL), lambda b, s: (b, s, 0))


def _seq_params():
    return pltpu.CompilerParams(dimension_semantics=("arbitrary", "arbitrary"),
                                vmem_limit_bytes=VMEM_LIMIT)


def _mem_kv_call(mem, norm_mem_g, wkv):
    B = mem.shape[0]
    bf16 = jnp.bfloat16
    return pl.pallas_call(
        _memkv_kernel,
        grid=(B,),
        in_specs=[pl.BlockSpec((1, MEM_LEN, D_MODEL), lambda b: (b, 0, 0)),
                  _const_spec((1, D_MODEL)), _const_spec((D_MODEL, 2 * D_MODEL))],
        out_specs=[pl.BlockSpec((1, MEM_LEN, D_MODEL), lambda b: (b, 0, 0))] * 2,
        out_shape=[jax.ShapeDtypeStruct((B, MEM_LEN, D_MODEL), bf16)] * 2,
        compiler_params=pltpu.CompilerParams(dimension_semantics=("arbitrary",),
                                             vmem_limit_bytes=VMEM_LIMIT),
        name="mem_kv",
    )(mem, _row(norm_mem_g), wkv.astype(bf16))


def _mixer_call(x, norm_g, w_in, b_gate_if, conv_w, conv_b, sinks, w_out):
    B, S, D = x.shape
    bf16 = jnp.bfloat16
    f32 = jnp.float32
    n_gate = 2 * MLSTM_HEADS
    w_in_main = w_in[:, :IN_MAIN].astype(bf16)
    w_gate = jnp.pad(w_in[:, IN_MAIN:], ((0, 0), (0, GATE_PAD - n_gate))).astype(bf16)
    b_gate = jnp.pad(b_gate_if.astype(f32), (0, GATE_PAD - n_gate)).reshape(1, GATE_PAD)
    slopes = tuple(float(2.0 ** (-8.0 * (i + 1) / ATT_HEADS)) for i in range(ATT_HEADS))
    return pl.pallas_call(
        functools.partial(_mixer_kernel, slopes=slopes),
        grid=(B, S // SEQ_TILE),
        in_specs=[pl.BlockSpec(memory_space=pltpu.SMEM),
                  _tile_spec(),
                  _const_spec((1, D)),
                  _const_spec((D, IN_MAIN)),
                  _const_spec((D, GATE_PAD)),
                  _const_spec((1, GATE_PAD)),
                  _const_spec((QK_CONV_WIDTH, 2 * MLSTM_WIDTH)),
                  _const_spec((1, 2 * MLSTM_WIDTH)),
                  _const_spec((D, D))],
        out_specs=_tile_spec(),
        out_shape=jax.ShapeDtypeStruct((B, S, D), f32),
        scratch_shapes=[
            pltpu.VMEM((WINDOW + SEQ_TILE, 2 * ATT_KV_WIDTH), bf16),
            pltpu.VMEM((CARRY_ROWS + SEQ_TILE, 2 * MLSTM_WIDTH), f32),
            pltpu.VMEM((MLSTM_HEADS, MLSTM_HEAD_DIM, 2 * MLSTM_HEAD_DIM), f32),
            pltpu.VMEM((CARRY_ROWS, GATE_PAD), f32),
            pltpu.VMEM((SEQ_TILE, D), bf16),
        ],
        compiler_params=_seq_params(),
        name="token_mixer",
    )(sinks.astype(f32), x, _row(norm_g), w_in_main, w_gate, b_gate,
      conv_w.astype(f32), _row(conv_b), w_out.astype(bf16))


def _ffn_call(x1, mem_k, mem_v, norm_x_g, wq, wo, norm_f_g, w_up, conv_w, conv_b, w_down,
              norm_final_g):
    B, S, D = x1.shape
    bf16 = jnp.bfloat16
    f32 = jnp.float32
    kv_spec = pl.BlockSpec((1, MEM_LEN, D), lambda b, s: (b, 0, 0))
    return pl.pallas_call(
        _ffn_kernel,
        grid=(B, S // SEQ_TILE),
        in_specs=[_tile_spec(), kv_spec, kv_spec,
                  _const_spec((1, D)), _const_spec((D, D)), _const_spec((D, D)),
                  _const_spec((1, D)), _const_spec((D, 2 * D_FF)),
                  _const_spec((FFN_CONV_WIDTH, D_FF)), _const_spec((1, D_FF)),
                  _const_spec((D_FF, D)), _const_spec((1, D))],
        out_specs=_tile_spec(),
        out_shape=jax.ShapeDtypeStruct((B, S, D), f32),
        scratch_shapes=[
            pltpu.VMEM((SEQ_TILE, D), bf16),
            pltpu.VMEM((CARRY_ROWS + SEQ_TILE, D_FF), f32),
            pltpu.VMEM((SEQ_TILE, D_FF), bf16),
        ],
        compiler_params=_seq_params(),
        name="xattn_ffn",
    )(x1, mem_k, mem_v, _row(norm_x_g), wq.astype(bf16), wo.astype(bf16), _row(norm_f_g),
      w_up.astype(bf16), conv_w.astype(f32), _row(conv_b), w_down.astype(bf16),
      _row(norm_final_g))


def kernel(x, mem, norm_mix_g, w_in, b_gate_if, conv_qk_w, conv_qk_b, attn_sinks, w_out,
           norm_xattn_g, norm_mem_g, wq_x, wkv_x, wo_x, norm_ffn_g, w_up, conv_ffn_w,
           conv_ffn_b, w_down, norm_final_g):
    assert x.shape[2] == D_MODEL and x.shape[1] % SEQ_TILE == 0 and w_in.shape[0] == 1
    mem_k, mem_v = _mem_kv_call(mem, norm_mem_g[0], wkv_x[0])
    x1 = _mixer_call(x, norm_mix_g[0], w_in[0], b_gate_if[0], conv_qk_w[0], conv_qk_b[0],
                     attn_sinks[0], w_out[0])
    return _ffn_call(x1, mem_k, mem_v, norm_xattn_g[0], wq_x[0], wo_x[0], norm_ffn_g[0],
                     w_up[0], conv_ffn_w[0], conv_ffn_b[0], w_down[0], norm_final_g)
```

```python
import functools

import jax
import jax.numpy as jnp
from jax import lax
from jax.experimental import pallas as pl
from jax.experimental.pallas import tpu as pltpu

D_MODEL = 1024
ATT_HEADS = 8
ATT_KV_HEADS = 2
ATT_GROUP = ATT_HEADS // ATT_KV_HEADS
ATT_HEAD_DIM = 64
ATT_WIDTH = ATT_HEADS * ATT_HEAD_DIM
ATT_KV_WIDTH = ATT_KV_HEADS * ATT_HEAD_DIM
WINDOW = 128
MLSTM_HEADS = 4
MLSTM_WIDTH = 512
MLSTM_HEAD_DIM = 128
QK_CONV_WIDTH = 4
X_HEADS = 4
X_HEAD_DIM = 256
MEM_LEN = 256
D_FF = 2816
FFN_CONV_WIDTH = 3
RMS_EPS = 1e-6
NEG_BIG = -1e30

O_AQ = 0
O_AK = ATT_WIDTH
O_AV = O_AK + ATT_KV_WIDTH
O_MQK = O_AV + ATT_KV_WIDTH
O_MV = O_MQK + 2 * MLSTM_WIDTH
O_MO = O_MV + MLSTM_WIDTH
IN_MAIN = O_MO + MLSTM_WIDTH
GATE_PAD = 256

SEQ_TILE = 512
MIX_ROWS = WINDOW
FFN_COLS = 256
SWA_TILES = 4
SWA_ROWS = 32
CARRY_ROWS = 8
VMEM_LIMIT = 56 * 1024 * 1024


def _rmsnorm(x, g):
    return x * lax.rsqrt(jnp.mean(x * x, axis=-1, keepdims=True) + RMS_EPS) * g


def _dot(a, b):
    return jnp.dot(a, b, preferred_element_type=jnp.float32)


def _dot_nt(a, b):
    return lax.dot_general(a, b, (((1,), (1,)), ((), ())),
                           preferred_element_type=jnp.float32)


def _log_sigmoid(x):
    return jnp.minimum(x, 0.0) - jnp.log1p(jnp.exp(-jnp.abs(x)))


def _sigmoid(x):
    return 1.0 / (1.0 + jnp.exp(-x))


def _memkv_kernel(mem_ref, g_ref, wkv_ref, k_ref, v_ref):
    hm = _rmsnorm(mem_ref[0], g_ref[...]).astype(jnp.bfloat16)
    kv = _dot(hm, wkv_ref[...])
    k_ref[0] = kv[:, :D_MODEL].astype(jnp.bfloat16)
    v_ref[0] = kv[:, D_MODEL:].astype(jnp.bfloat16)


def _band_bias(slopes, first):
    W = WINDOW
    qi = lax.broadcasted_iota(jnp.int32, (W, 2 * W), 0)
    kj = lax.broadcasted_iota(jnp.int32, (W, 2 * W), 1)
    dist = qi + W - kj
    valid = (dist >= 0) & (dist < W)
    if first:
        valid = valid & (kj >= W)
    distf = dist.astype(jnp.float32)
    return [jnp.where(valid, -(sl * distf), NEG_BIG) for sl in slopes]


def _cumsum_rows(tril, v, lane):
    n = MLSTM_HEADS
    hi = v.astype(jnp.bfloat16)
    r1 = v - hi.astype(jnp.float32)
    mid = r1.astype(jnp.bfloat16)
    lo = (r1 - mid.astype(jnp.float32)).astype(jnp.bfloat16)
    packed = jnp.where(lane < n, hi.astype(jnp.float32),
                       jnp.where(lane < 2 * n, pltpu.roll(mid.astype(jnp.float32), n, axis=1),
                                 jnp.where(lane < 3 * n, pltpu.roll(lo.astype(jnp.float32), 2 * n, axis=1),
                                           0.0)))
    r = _dot(tril, packed.astype(jnp.bfloat16))
    return r + pltpu.roll(r, 128 - n, axis=1) + pltpu.roll(r, 128 - 2 * n, axis=1)


def _mixer_kernel(sinks_ref, x_ref, g_ref, win_ref, wgate_ref, bgate_ref, convw_ref,
                  convb_ref, wout_ref, o_ref,
                  kv_buf_ref, qk_buf_ref, c_ref, m_ref, bias_ref, bias0_ref, *, slopes):
    s_idx = pl.program_id(1)
    ts = SEQ_TILE
    W = WINDOW
    L = MIX_ROWS
    bf16 = jnp.bfloat16

    @pl.when(s_idx == 0)
    def _():
        kv_buf_ref[:, 0:W, :] = jnp.zeros((2 * SWA_TILES, W, 128), bf16)
        qk_buf_ref[0:CARRY_ROWS, :] = jnp.zeros((CARRY_ROWS, 2 * MLSTM_WIDTH), jnp.float32)
        c_ref[...] = jnp.zeros_like(c_ref)
        m_ref[...] = jnp.zeros_like(m_ref)
        for head, (bias, bias0) in enumerate(zip(_band_bias(slopes, False), _band_bias(slopes, True))):
            bias_ref[head] = bias
            bias0_ref[head] = bias0

    @pl.when(s_idx == 1)
    def _():
        bias0_ref[...] = bias_ref[...]

    @pl.when(s_idx > 0)
    def _():
        kv_buf_ref[:, 0:W, :] = kv_buf_ref[:, ts:ts + W, :]
        qk_buf_ref[0:CARRY_ROWS, :] = qk_buf_ref[ts:ts + CARRY_ROWS, :]

    lane = lax.broadcasted_iota(jnp.int32, (L, 128), 1)
    low = lane < ATT_HEAD_DIM
    zero_tile = jnp.zeros((L, 128), bf16)
    lane_rc = lax.broadcasted_iota(jnp.int32, (SWA_ROWS, 128), 1)
    row = lax.broadcasted_iota(jnp.int32, (L, L), 0)
    col = lax.broadcasted_iota(jnp.int32, (L, L), 1)
    causal = row >= col
    tril = causal.astype(bf16)
    ones_col = (lane == 0).astype(bf16)

    def project(blk):
        x = x_ref[0, blk * L:(blk + 1) * L, :]
        h = _rmsnorm(x, g_ref[...]).astype(bf16)
        p = _dot(h, win_ref[...])
        gates = _dot(h, wgate_ref[...]) + bgate_ref[...]
        return x, p, gates

    n_blk = ts // L
    projected = project(0)
    for blk in range(n_blk):
        r0 = blk * L
        x, p, gates = projected

        for which, col0 in enumerate((O_AK, O_AV)):
            t32 = p[:, col0:col0 + ATT_KV_WIDTH]
            tb = t32.astype(bf16)
            rb = pltpu.roll(t32, ATT_HEAD_DIM, axis=1).astype(bf16)
            base = which * SWA_TILES
            kv_buf_ref[base + 0, W + r0:W + r0 + L, :] = jnp.where(low, tb, zero_tile)
            kv_buf_ref[base + 1, W + r0:W + r0 + L, :] = jnp.where(low, zero_tile, rb)
            kv_buf_ref[base + 2, W + r0:W + r0 + L, :] = jnp.where(low, rb, zero_tile)
            kv_buf_ref[base + 3, W + r0:W + r0 + L, :] = jnp.where(low, zero_tile, tb)

        q_all = (p[:, O_AQ:O_AQ + ATT_WIDTH] * (ATT_HEAD_DIM ** -0.5)).astype(bf16)
        bref = bias0_ref if blk == 0 else bias_ref
        out_tiles = []
        scores = []
        for hk in range(ATT_KV_HEADS):
            qq = jnp.concatenate([q_all[:, hk * 256:hk * 256 + 128],
                                  q_all[:, hk * 256 + 128:hk * 256 + 256]], axis=0)
            scores.append([_dot_nt(qq, kv_buf_ref[2 * hk + par, r0:r0 + 2 * W, :]) for par in range(2)])
        if blk + 1 < n_blk:
            projected = project(blk + 1)
        for hk in range(ATT_KV_HEADS):
            s_par = scores[hk]
            e_rows = [[], []]
            scale_rows = []
            for half in range(2):
                for rc in range(W // SWA_ROWS):
                    rr = rc * SWA_ROWS
                    rden = []
                    for par in range(2):
                        head = ATT_GROUP * hk + 2 * half + par
                        sc = (s_par[par][half * W + rr:half * W + rr + SWA_ROWS]
                              + bref[head, rr:rr + SWA_ROWS, :])
                        sink = sinks_ref[head]
                        mx = jnp.maximum(jnp.max(sc, axis=-1, keepdims=True), sink)
                        e = jnp.exp(sc - mx)
                        den = jnp.sum(e, axis=-1, keepdims=True) + jnp.exp(sink - mx)
                        e_rows[par].append(e.astype(bf16))
                        rden.append(1.0 / den)
                    scale_rows.append(jnp.where(lane_rc < ATT_HEAD_DIM, rden[0], rden[1]))
            e_all = jnp.concatenate([jnp.concatenate(e_rows[0], axis=0),
                                     jnp.concatenate(e_rows[1], axis=0)], axis=1)
            vz = jnp.concatenate([kv_buf_ref[SWA_TILES + 2 * hk, r0:r0 + 2 * W, :],
                                  kv_buf_ref[SWA_TILES + 2 * hk + 1, r0:r0 + 2 * W, :]], axis=0)
            o = (_dot(e_all, vz) * jnp.concatenate(scale_rows, axis=0)).astype(bf16)
            out_tiles += [o[0:W], o[W:2 * W]]

        qk_buf_ref[CARRY_ROWS + r0:CARRY_ROWS + r0 + L, :] = p[:, O_MQK:O_MQK + 2 * MLSTM_WIDTH]
        conv = convb_ref[...]
        for j in range(QK_CONV_WIDTH):
            start = CARRY_ROWS + r0 - (QK_CONV_WIDTH - 1 - j)
            conv = conv + convw_ref[j:j + 1, :] * qk_buf_ref[start:start + L, :]
        mqk = conv * _sigmoid(conv)

        ig = gates[:, 0:128]
        lf = _log_sigmoid(gates[:, 128:256])
        b = _cumsum_rows(tril, lf, lane)
        a = ig - b
        a_t = a.T
        m_prev = m_ref[0:1, :]
        dmats = []
        rowmax = None
        for hd in range(MLSTM_HEADS):
            dm = jnp.where(causal, b[:, hd:hd + 1] + a_t[hd:hd + 1, :], NEG_BIG)
            dmats.append(dm)
            rm = jnp.max(dm, axis=-1, keepdims=True)
            rowmax = jnp.broadcast_to(rm, (L, 128)) if hd == 0 else jnp.where(lane == hd, rm, rowmax)
        inter = b + m_prev
        m_t = jnp.maximum(inter, rowmax)
        w_inter = jnp.exp(inter - m_t)
        e_negm = jnp.exp(-m_t)
        b_end = b[L - 1:L, :]
        g = b_end + a
        m_new = jnp.maximum(b_end + m_prev, jnp.max(g, axis=0, keepdims=True))
        decay = jnp.exp(b_end + m_prev - m_new)
        ws = jnp.exp(g - m_new)
        m_ref[0:1, :] = m_new
        for hd in range(MLSTM_HEADS):
            c0 = hd * MLSTM_HEAD_DIM
            dexp = jnp.exp(dmats[hd] - m_t[:, hd:hd + 1])
            qf = mqk[:, c0:c0 + MLSTM_HEAD_DIM] * (MLSTM_HEAD_DIM ** -0.5)
            kf = mqk[:, MLSTM_WIDTH + c0:MLSTM_WIDTH + c0 + MLSTM_HEAD_DIM]
            qb = qf.astype(bf16)
            kb = kf.astype(bf16)
            vb = p[:, O_MV + c0:O_MV + c0 + MLSTM_HEAD_DIM].astype(bf16)
            v_ext = jnp.concatenate([vb, ones_col], axis=1)
            sc = _dot_nt(qb, kb) * dexp
            c_old = c_ref[hd]
            qw = (qf * w_inter[:, hd:hd + 1]).astype(bf16)
            num_ext = _dot(jnp.concatenate([qw, sc.astype(bf16)], axis=1),
                           jnp.concatenate([c_old.astype(bf16), v_ext], axis=0))
            num = num_ext[:, :MLSTM_HEAD_DIM]
            den = num_ext[:, MLSTM_HEAD_DIM:MLSTM_HEAD_DIM + 1]
            hh = num * (1.0 / jnp.maximum(jnp.abs(den), e_negm[:, hd:hd + 1]))
            og = _sigmoid(p[:, O_MO + c0:O_MO + c0 + MLSTM_HEAD_DIM])
            out_tiles.append((og * hh).astype(bf16))
            kwt = (kf * ws[:, hd:hd + 1]).T.astype(bf16)
            c_ref[hd] = decay[0:1, hd:hd + 1] * c_old + _dot(kwt, v_ext)

        o_ref[0, r0:r0 + L, :] = x + _dot(jnp.concatenate(out_tiles, axis=1), wout_ref[...])


def _ffn_kernel(x_ref, k_ref, v_ref, gx_ref, wq_ref, wo_ref, gf_ref, wup_ref, convw_ref,
                convb_ref, wdown_ref, gfin_ref, o_ref, cat_ref, u_buf_ref, act_ref):
    s_idx = pl.program_id(1)
    ts = SEQ_TILE

    @pl.when(s_idx == 0)
    def _():
        u_buf_ref[0:CARRY_ROWS, :] = jnp.zeros((CARRY_ROWS, D_FF), jnp.float32)

    @pl.when(s_idx > 0)
    def _():
        u_buf_ref[0:CARRY_ROWS, :] = u_buf_ref[ts:ts + CARRY_ROWS, :]

    x1 = x_ref[0]
    hq = _rmsnorm(x1, gx_ref[...]).astype(jnp.bfloat16)
    q = (_dot(hq, wq_ref[...]) * (X_HEAD_DIM ** -0.5)).astype(jnp.bfloat16)
    for hd in range(X_HEADS):
        c0 = hd * X_HEAD_DIM
        sc = _dot_nt(q[:, c0:c0 + X_HEAD_DIM], k_ref[0, :, c0:c0 + X_HEAD_DIM])
        e = jnp.exp(sc - jnp.max(sc, axis=-1, keepdims=True))
        den = jnp.sum(e, axis=-1, keepdims=True)
        o = _dot(e.astype(jnp.bfloat16), v_ref[0, :, c0:c0 + X_HEAD_DIM]) / den
        cat_ref[:, c0:c0 + X_HEAD_DIM] = o.astype(jnp.bfloat16)
    x2 = x1 + _dot(cat_ref[...], wo_ref[...])

    hf = _rmsnorm(x2, gf_ref[...]).astype(jnp.bfloat16)
    for j in range(D_FF // FFN_COLS):
        c0 = j * FFN_COLS
        u_buf_ref[CARRY_ROWS:CARRY_ROWS + ts, c0:c0 + FFN_COLS] = _dot(hf, wup_ref[:, c0:c0 + FFN_COLS])
        up = _dot(hf, wup_ref[:, D_FF + c0:D_FF + c0 + FFN_COLS])
        g = convb_ref[:, c0:c0 + FFN_COLS]
        for t in range(FFN_CONV_WIDTH):
            shift = FFN_CONV_WIDTH - 1 - t
            g = g + (convw_ref[t:t + 1, c0:c0 + FFN_COLS]
                     * u_buf_ref[CARRY_ROWS - shift:CARRY_ROWS - shift + ts, c0:c0 + FFN_COLS])
        act_ref[:, c0:c0 + FFN_COLS] = (g * _sigmoid(g) * up).astype(jnp.bfloat16)
    x3 = x2 + _dot(act_ref[...], wdown_ref[...])
    o_ref[0] = _rmsnorm(x3, gfin_ref[...])


def _const_spec(shape):
    return pl.BlockSpec(shape, lambda *_: (0,) * len(shape), pipeline_mode=pl.Buffered(1))


def _row(a):
    return a.reshape(1, -1).astype(jnp.float32)


def _tile_spec():
    return pl.BlockSpec((1, SEQ_TILE, D_MODEL), lambda b, s: (b, s, 0))


def _seq_params():
    return pltpu.CompilerParams(dimension_semantics=("arbitrary", "arbitrary"),
                                vmem_limit_bytes=VMEM_LIMIT)


def _mem_kv_call(mem, norm_mem_g, wkv):
    B = mem.shape[0]
    bf16 = jnp.bfloat16
    return pl.pallas_call(
        _memkv_kernel,
        grid=(B,),
        in_specs=[pl.BlockSpec((1, MEM_LEN, D_MODEL), lambda b: (b, 0, 0)),
                  _const_spec((1, D_MODEL)), _const_spec((D_MODEL, 2 * D_MODEL))],
        out_specs=[pl.BlockSpec((1, MEM_LEN, D_MODEL), lambda b: (b, 0, 0))] * 2,
        out_shape=[jax.ShapeDtypeStruct((B, MEM_LEN, D_MODEL), bf16)] * 2,
        compiler_params=pltpu.CompilerParams(dimension_semantics=("arbitrary",),
                                             vmem_limit_bytes=VMEM_LIMIT),
        name="mem_kv",
    )(mem, _row(norm_mem_g), wkv.astype(bf16))


def _mixer_call(x, norm_g, w_in, b_gate_if, conv_w, conv_b, sinks, w_out):
    B, S, D = x.shape
    bf16 = jnp.bfloat16
    f32 = jnp.float32
    w_in_main = w_in[:, :IN_MAIN].astype(bf16)
    H = MLSTM_HEADS
    w_gate = jnp.zeros((D, GATE_PAD), f32)
    w_gate = w_gate.at[:, 0:H].set(w_in[:, IN_MAIN:IN_MAIN + H])
    w_gate = w_gate.at[:, 128:128 + H].set(w_in[:, IN_MAIN + H:IN_MAIN + 2 * H]).astype(bf16)
    b_gate = jnp.zeros((1, GATE_PAD), f32)
    b_gate = b_gate.at[0, 0:H].set(b_gate_if[0:H].astype(f32))
    b_gate = b_gate.at[0, 128:128 + H].set(b_gate_if[H:2 * H].astype(f32))
    slopes = tuple(float(2.0 ** (-8.0 * (i + 1) / ATT_HEADS)) for i in range(ATT_HEADS))
    return pl.pallas_call(
        functools.partial(_mixer_kernel, slopes=slopes),
        grid=(B, S // SEQ_TILE),
        in_specs=[pl.BlockSpec(memory_space=pltpu.SMEM),
                  _tile_spec(),
                  _const_spec((1, D)),
                  _const_spec((D, IN_MAIN)),
                  _const_spec((D, GATE_PAD)),
                  _const_spec((1, GATE_PAD)),
                  _const_spec((QK_CONV_WIDTH, 2 * MLSTM_WIDTH)),
                  _const_spec((1, 2 * MLSTM_WIDTH)),
                  _const_spec((D, D))],
        out_specs=_tile_spec(),
        out_shape=jax.ShapeDtypeStruct((B, S, D), f32),
        scratch_shapes=[
            pltpu.VMEM((2 * SWA_TILES, WINDOW + SEQ_TILE, 128), bf16),
            pltpu.VMEM((CARRY_ROWS + SEQ_TILE, 2 * MLSTM_WIDTH), f32),
            pltpu.VMEM((MLSTM_HEADS, MLSTM_HEAD_DIM, 2 * MLSTM_HEAD_DIM), f32),
            pltpu.VMEM((CARRY_ROWS, 128), f32),
            pltpu.VMEM((ATT_HEADS, WINDOW, 2 * WINDOW), f32),
            pltpu.VMEM((ATT_HEADS, WINDOW, 2 * WINDOW), f32),
        ],
        compiler_params=_seq_params(),
        name="token_mixer",
    )(sinks.astype(f32), x, _row(norm_g), w_in_main, w_gate, b_gate,
      conv_w.astype(f32), _row(conv_b), w_out.astype(bf16))


def _ffn_call(x1, mem_k, mem_v, norm_x_g, wq, wo, norm_f_g, w_up, conv_w, conv_b, w_down,
              norm_final_g):
    B, S, D = x1.shape
    bf16 = jnp.bfloat16
    f32 = jnp.float32
    kv_spec = pl.BlockSpec((1, MEM_LEN, D), lambda b, s: (b, 0, 0))
    return pl.pallas_call(
        _ffn_kernel,
        grid=(B, S // SEQ_TILE),
        in_specs=[_tile_spec(), kv_spec, kv_spec,
                  _const_spec((1, D)), _const_spec((D, D)), _const_spec((D, D)),
                  _const_spec((1, D)), _const_spec((D, 2 * D_FF)),
                  _const_spec((FFN_CONV_WIDTH, D_FF)), _const_spec((1, D_FF)),
                  _const_spec((D_FF, D)), _const_spec((1, D))],
        out_specs=_tile_spec(),
        out_shape=jax.ShapeDtypeStruct((B, S, D), f32),
        scratch_shapes=[
            pltpu.VMEM((SEQ_TILE, D), bf16),
            pltpu.VMEM((CARRY_ROWS + SEQ_TILE, D_FF), f32),
            pltpu.VMEM((SEQ_TILE, D_FF), bf16),
        ],
        compiler_params=_seq_params(),
        name="xattn_ffn",
    )(x1, mem_k, mem_v, _row(norm_x_g), wq.astype(bf16), wo.astype(bf16), _row(norm_f_g),
      w_up.astype(bf16), conv_w.astype(f32), _row(conv_b), w_down.astype(bf16),
      _row(norm_final_g))


def kernel(x, mem, norm_mix_g, w_in, b_gate_if, conv_qk_w, conv_qk_b, attn_sinks, w_out,
           norm_xattn_g, norm_mem_g, wq_x, wkv_x, wo_x, norm_ffn_g, w_up, conv_ffn_w,
           conv_ffn_b, w_down, norm_final_g):
    assert x.shape[2] == D_MODEL and x.shape[1] % SEQ_TILE == 0 and w_in.shape[0] == 1
    mem_k, mem_v = _mem_kv_call(mem, norm_mem_g[0], wkv_x[0])
    x1 = _mixer_call(x, norm_mix_g[0], w_in[0], b_gate_if[0], conv_qk_w[0], conv_qk_b[0],
                     attn_sinks[0], w_out[0])
    return _ffn_call(x1, mem_k, mem_v, norm_xattn_g[0], wq_x[0], wo_x[0], norm_ffn_g[0],
                     w_up[0], conv_ffn_w[0], conv_ffn_b[0], w_down[0], norm_final_g)
```

```python
import functools

import jax
import jax.numpy as jnp
from jax import lax
from jax.experimental import pallas as pl
from jax.experimental.pallas import tpu as pltpu

D_MODEL = 1024
ATT_HEADS = 8
ATT_KV_HEADS = 2
ATT_GROUP = ATT_HEADS // ATT_KV_HEADS
ATT_HEAD_DIM = 64
ATT_WIDTH = ATT_HEADS * ATT_HEAD_DIM
ATT_KV_WIDTH = ATT_KV_HEADS * ATT_HEAD_DIM
WINDOW = 128
MLSTM_HEADS = 4
MLSTM_WIDTH = 512
MLSTM_HEAD_DIM = 128
QK_CONV_WIDTH = 4
X_HEADS = 4
X_HEAD_DIM = 256
MEM_LEN = 256
D_FF = 2816
FFN_CONV_WIDTH = 3
RMS_EPS = 1e-6
NEG_BIG = -1e30

O_AQ = 0
O_AK = ATT_WIDTH
O_AV = O_AK + ATT_KV_WIDTH
O_MQK = O_AV + ATT_KV_WIDTH
O_MV = O_MQK + 2 * MLSTM_WIDTH
O_MO = O_MV + MLSTM_WIDTH
IN_MAIN = O_MO + MLSTM_WIDTH
GATE_PAD = 256

SEQ_TILE = 512
MIX_TILE = 1024
MIX_SEQS = 1
MIX_ROWS = WINDOW
PROJ_COLS = 256
OUT_COLS = 512
FFN_COLS = 256
SWA_TILES = 4
SWA_ROWS = 32
CARRY_ROWS = 8
VMEM_LIMIT = 56 * 1024 * 1024


def _rmsnorm(x, g):
    return x * lax.rsqrt(jnp.mean(x * x, axis=-1, keepdims=True) + RMS_EPS) * g


def _dot(a, b):
    return jnp.dot(a, b, preferred_element_type=jnp.float32)


def _dot_nt(a, b):
    return lax.dot_general(a, b, (((1,), (1,)), ((), ())),
                           preferred_element_type=jnp.float32)


def _log_sigmoid(x):
    return jnp.minimum(x, 0.0) - jnp.log1p(jnp.exp(-jnp.abs(x)))


def _sigmoid(x):
    return 1.0 / (1.0 + jnp.exp(-x))


def _memkv_kernel(mem_ref, g_ref, wkv_ref, k_ref, v_ref):
    hm = _rmsnorm(mem_ref[0], g_ref[...]).astype(jnp.bfloat16)
    kv = _dot(hm, wkv_ref[...])
    k_ref[0] = kv[:, :D_MODEL].astype(jnp.bfloat16)
    v_ref[0] = kv[:, D_MODEL:].astype(jnp.bfloat16)


def _band_bias(slopes, first):
    W = WINDOW
    qi = lax.broadcasted_iota(jnp.int32, (W, 2 * W), 0)
    kj = lax.broadcasted_iota(jnp.int32, (W, 2 * W), 1)
    dist = qi + W - kj
    valid = (dist >= 0) & (dist < W)
    if first:
        valid = valid & (kj >= W)
    distf = dist.astype(jnp.float32)
    return [jnp.where(valid, -(sl * distf), NEG_BIG) for sl in slopes]


def _cumsum_rows(tril, v, lane):
    n = MLSTM_HEADS
    hi = v.astype(jnp.bfloat16)
    r1 = v - hi.astype(jnp.float32)
    mid = r1.astype(jnp.bfloat16)
    lo = (r1 - mid.astype(jnp.float32)).astype(jnp.bfloat16)
    packed = jnp.where(lane < n, hi.astype(jnp.float32),
                       jnp.where(lane < 2 * n, pltpu.roll(mid.astype(jnp.float32), n, axis=1),
                                 jnp.where(lane < 3 * n, pltpu.roll(lo.astype(jnp.float32), 2 * n, axis=1),
                                           0.0)))
    r = _dot(tril, packed.astype(jnp.bfloat16))
    return r + pltpu.roll(r, 128 - n, axis=1) + pltpu.roll(r, 128 - 2 * n, axis=1)


def _alternate(*stage_iters):
    live = list(stage_iters)
    while live:
        live = [it for it in live if next(it, _DONE) is not _DONE]


_DONE = object()


def _mixer_kernel(sinks_ref, x_ref, g_ref, win_ref, wgate_ref, bgate_ref, convw_ref,
                  convb_ref, wout_ref, o_ref,
                  kv_buf_ref, qk_buf_ref, c_ref, m_ref, bias_ref, bias0_ref,
                  *step_refs, slopes):
    n = len(step_refs) // MIX_SEQS
    for bi in range(MIX_SEQS):
        _mixer_sequence(pl.program_id(1), sinks_ref, x_ref.at[bi], g_ref, win_ref, wgate_ref,
                        bgate_ref, convw_ref, convb_ref, wout_ref, o_ref.at[bi],
                        kv_buf_ref.at[bi], qk_buf_ref.at[bi], c_ref.at[bi], m_ref.at[bi],
                        bias_ref, bias0_ref, step_refs[bi * n:(bi + 1) * n], slopes)


def _mixer_sequence(s_idx, sinks_ref, x_ref, g_ref, win_ref, wgate_ref, bgate_ref, convw_ref,
                    convb_ref, wout_ref, o_ref,
                    kv_buf_ref, qk_buf_ref, c_ref, m_ref, bias_ref, bias0_ref,
                    step_refs, slopes):
    ts = MIX_TILE
    W = WINDOW
    L = MIX_ROWS
    bf16 = jnp.bfloat16

    @pl.when(s_idx == 0)
    def _():
        kv_buf_ref[:, 0:W, :] = jnp.zeros((2 * SWA_TILES, W, 128), bf16)
        qk_buf_ref[0:CARRY_ROWS, :] = jnp.zeros((CARRY_ROWS, 2 * MLSTM_WIDTH), jnp.float32)
        c_ref[...] = jnp.zeros_like(c_ref)
        m_ref[...] = jnp.zeros_like(m_ref)
        for head, (bias, bias0) in enumerate(zip(_band_bias(slopes, False), _band_bias(slopes, True))):
            bias_ref[head] = bias
            bias0_ref[head] = bias0

    @pl.when(s_idx == 1)
    def _():
        bias0_ref[...] = bias_ref[...]

    @pl.when(s_idx > 0)
    def _():
        kv_buf_ref[:, 0:W, :] = kv_buf_ref[:, ts:ts + W, :]
        qk_buf_ref[0:CARRY_ROWS, :] = qk_buf_ref[ts:ts + CARRY_ROWS, :]

    lane = lax.broadcasted_iota(jnp.int32, (L, 128), 1)
    low = lane < ATT_HEAD_DIM
    zero_tile = jnp.zeros((L, 128), bf16)
    lane_rc = lax.broadcasted_iota(jnp.int32, (SWA_ROWS, 128), 1)
    row = lax.broadcasted_iota(jnp.int32, (L, L), 0)
    col = lax.broadcasted_iota(jnp.int32, (L, L), 1)
    causal = row >= col
    tril = causal.astype(bf16)
    ones_col = (lane == 0).astype(bf16)

    def project(blk):
        r0 = blk * L
        q_ref, mqk_ref, vb_ref, og_ref, gates_ref = step_refs[5 * blk:5 * blk + 5]
        x = x_ref[r0:r0 + L, :]
        h = _rmsnorm(x, g_ref[...]).astype(bf16)

        def cols(c0, width):
            return _dot(h, win_ref[:, c0:c0 + width])

        for c0 in range(0, 2 * MLSTM_WIDTH, PROJ_COLS):
            qk_buf_ref[CARRY_ROWS + r0:CARRY_ROWS + r0 + L, c0:c0 + PROJ_COLS] = cols(O_MQK + c0, PROJ_COLS)
            conv = convb_ref[:, c0:c0 + PROJ_COLS]
            for j in range(QK_CONV_WIDTH):
                start = CARRY_ROWS + r0 - (QK_CONV_WIDTH - 1 - j)
                conv = conv + (convw_ref[j:j + 1, c0:c0 + PROJ_COLS]
                               * qk_buf_ref[start:start + L, c0:c0 + PROJ_COLS])
            mqk_ref[:, c0:c0 + PROJ_COLS] = conv * _sigmoid(conv)
            yield
        kv = cols(O_AK, 2 * ATT_KV_WIDTH)
        for which in range(2):
            t32 = kv[:, which * ATT_KV_WIDTH:(which + 1) * ATT_KV_WIDTH]
            tb = t32.astype(bf16)
            rb = pltpu.roll(t32, ATT_HEAD_DIM, axis=1).astype(bf16)
            base = which * SWA_TILES
            kv_buf_ref[base + 0, W + r0:W + r0 + L, :] = jnp.where(low, tb, zero_tile)
            kv_buf_ref[base + 1, W + r0:W + r0 + L, :] = jnp.where(low, zero_tile, rb)
            kv_buf_ref[base + 2, W + r0:W + r0 + L, :] = jnp.where(low, rb, zero_tile)
            kv_buf_ref[base + 3, W + r0:W + r0 + L, :] = jnp.where(low, zero_tile, tb)
        yield
        q_ref[...] = (cols(O_AQ, ATT_WIDTH) * (ATT_HEAD_DIM ** -0.5)).astype(bf16)
        yield
        vb_ref[...] = cols(O_MV, MLSTM_WIDTH).astype(bf16)
        yield
        og_ref[...] = _sigmoid(cols(O_MO, MLSTM_WIDTH))
        yield
        gates_ref[...] = _dot(h, wgate_ref[...]) + bgate_ref[...]

    def rest(blk):
        r0 = blk * L
        x = x_ref[r0:r0 + L, :]
        q_ref, mqk_ref, vb_ref, og_ref, gates_ref = step_refs[5 * blk:5 * blk + 5]

        bref = bias0_ref if blk == 0 else bias_ref
        out_tiles = []
        scores = []
        for hk in range(ATT_KV_HEADS):
            qq = jnp.concatenate([q_ref[:, hk * 256:hk * 256 + 128],
                                  q_ref[:, hk * 256 + 128:hk * 256 + 256]], axis=0)
            scores.append([_dot_nt(qq, kv_buf_ref[2 * hk + par, r0:r0 + 2 * W, :]) for par in range(2)])
        yield
        for hk in range(ATT_KV_HEADS):
            s_par = scores[hk]
            e_rows = [[], []]
            scale_rows = []
            for half in range(2):
                for rc in range(W // SWA_ROWS):
                    rr = rc * SWA_ROWS
                    rden = []
                    for par in range(2):
                        head = ATT_GROUP * hk + 2 * half + par
                        sc = (s_par[par][half * W + rr:half * W + rr + SWA_ROWS]
                              + bref[head, rr:rr + SWA_ROWS, :])
                        sink = sinks_ref[head]
                        mx = jnp.maximum(jnp.max(sc, axis=-1, keepdims=True), sink)
                        e = jnp.exp(sc - mx)
                        den = jnp.sum(e, axis=-1, keepdims=True) + jnp.exp(sink - mx)
                        e_rows[par].append(e.astype(bf16))
                        rden.append(1.0 / den)
                    scale_rows.append(jnp.where(lane_rc < ATT_HEAD_DIM, rden[0], rden[1]))
                if half == 0:
                    yield
            e_all = jnp.concatenate([jnp.concatenate(e_rows[0], axis=0),
                                     jnp.concatenate(e_rows[1], axis=0)], axis=1)
            vz = jnp.concatenate([kv_buf_ref[SWA_TILES + 2 * hk, r0:r0 + 2 * W, :],
                                  kv_buf_ref[SWA_TILES + 2 * hk + 1, r0:r0 + 2 * W, :]], axis=0)
            o = (_dot(e_all, vz) * jnp.concatenate(scale_rows, axis=0)).astype(bf16)
            out_tiles += [o[0:W], o[W:2 * W]]
            yield

        ig = gates_ref[:, 0:128]
        lf = _log_sigmoid(gates_ref[:, 128:256])
        b = _cumsum_rows(tril, lf, lane)
        a = ig - b
        a_t = a.T
        m_prev = m_ref[0:1, :]
        dmats = []
        rowmax = None
        for hd in range(MLSTM_HEADS):
            dm = jnp.where(causal, b[:, hd:hd + 1] + a_t[hd:hd + 1, :], NEG_BIG)
            dmats.append(dm)
            rm = jnp.max(dm, axis=-1, keepdims=True)
            rowmax = jnp.broadcast_to(rm, (L, 128)) if hd == 0 else jnp.where(lane == hd, rm, rowmax)
        inter = b + m_prev
        m_t = jnp.maximum(inter, rowmax)
        w_inter = jnp.exp(inter - m_t)
        e_negm = jnp.exp(-m_t)
        b_end = b[L - 1:L, :]
        g = b_end + a
        m_new = jnp.maximum(b_end + m_prev, jnp.max(g, axis=0, keepdims=True))
        decay = jnp.exp(b_end + m_prev - m_new)
        ws = jnp.exp(g - m_new)
        m_ref[0:1, :] = m_new
        yield
        for hd in range(MLSTM_HEADS):
            c0 = hd * MLSTM_HEAD_DIM
            dexp = jnp.exp(dmats[hd] - m_t[:, hd:hd + 1])
            qf = mqk_ref[:, c0:c0 + MLSTM_HEAD_DIM] * (MLSTM_HEAD_DIM ** -0.5)
            kf = mqk_ref[:, MLSTM_WIDTH + c0:MLSTM_WIDTH + c0 + MLSTM_HEAD_DIM]
            qb = qf.astype(bf16)
            kb = kf.astype(bf16)
            vb = vb_ref[:, c0:c0 + MLSTM_HEAD_DIM]
            v_ext = jnp.concatenate([vb, ones_col], axis=1)
            sc = _dot_nt(qb, kb) * dexp
            c_old = c_ref[hd]
            qw = (qf * w_inter[:, hd:hd + 1]).astype(bf16)
            num_ext = _dot(jnp.concatenate([qw, sc.astype(bf16)], axis=1),
                           jnp.concatenate([c_old.astype(bf16), v_ext], axis=0))
            num = num_ext[:, :MLSTM_HEAD_DIM]
            den = num_ext[:, MLSTM_HEAD_DIM:MLSTM_HEAD_DIM + 1]
            hh = num * (1.0 / jnp.maximum(jnp.abs(den), e_negm[:, hd:hd + 1]))
            og = og_ref[:, c0:c0 + MLSTM_HEAD_DIM]
            out_tiles.append((og * hh).astype(bf16))
            kwt = (kf * ws[:, hd:hd + 1]).T.astype(bf16)
            c_ref[hd] = decay[0:1, hd:hd + 1] * c_old + _dot(kwt, v_ext)
            yield

        cat = jnp.concatenate(out_tiles, axis=1)
        for c0 in range(0, D_MODEL, OUT_COLS):
            o_ref[r0:r0 + L, c0:c0 + OUT_COLS] = x[:, c0:c0 + OUT_COLS] + _dot(cat, wout_ref[:, c0:c0 + OUT_COLS])
            yield

    n_blk = ts // L
    _alternate(project(0))
    for blk in range(n_blk):
        _alternate(rest(blk), project(blk + 1) if blk + 1 < n_blk else iter(()))


def _ffn_kernel(x_ref, k_ref, v_ref, gx_ref, wq_ref, wo_ref, gf_ref, wup_ref, convw_ref,
                convb_ref, wdown_ref, gfin_ref, o_ref, cat_ref, u_buf_ref, act_ref):
    s_idx = pl.program_id(1)
    ts = SEQ_TILE

    @pl.when(s_idx == 0)
    def _():
        u_buf_ref[0:CARRY_ROWS, :] = jnp.zeros((CARRY_ROWS, D_FF), jnp.float32)

    @pl.when(s_idx > 0)
    def _():
        u_buf_ref[0:CARRY_ROWS, :] = u_buf_ref[ts:ts + CARRY_ROWS, :]

    x1 = x_ref[0]
    hq = _rmsnorm(x1, gx_ref[...]).astype(jnp.bfloat16)
    q = (_dot(hq, wq_ref[...]) * (X_HEAD_DIM ** -0.5)).astype(jnp.bfloat16)
    for hd in range(X_HEADS):
        c0 = hd * X_HEAD_DIM
        sc = _dot_nt(q[:, c0:c0 + X_HEAD_DIM], k_ref[0, :, c0:c0 + X_HEAD_DIM])
        e = jnp.exp(sc - jnp.max(sc, axis=-1, keepdims=True))
        den = jnp.sum(e, axis=-1, keepdims=True)
        o = _dot(e.astype(jnp.bfloat16), v_ref[0, :, c0:c0 + X_HEAD_DIM]) / den
        cat_ref[:, c0:c0 + X_HEAD_DIM] = o.astype(jnp.bfloat16)
    x2 = x1 + _dot(cat_ref[...], wo_ref[...])

    hf = _rmsnorm(x2, gf_ref[...]).astype(jnp.bfloat16)
    for j in range(D_FF // FFN_COLS):
        c0 = j * FFN_COLS
        u_buf_ref[CARRY_ROWS:CARRY_ROWS + ts, c0:c0 + FFN_COLS] = _dot(hf, wup_ref[:, c0:c0 + FFN_COLS])
        up = _dot(hf, wup_ref[:, D_FF + c0:D_FF + c0 + FFN_COLS])
        g = convb_ref[:, c0:c0 + FFN_COLS]
        for t in range(FFN_CONV_WIDTH):
            shift = FFN_CONV_WIDTH - 1 - t
            g = g + (convw_ref[t:t + 1, c0:c0 + FFN_COLS]
                     * u_buf_ref[CARRY_ROWS - shift:CARRY_ROWS - shift + ts, c0:c0 + FFN_COLS])
        act_ref[:, c0:c0 + FFN_COLS] = (g * _sigmoid(g) * up).astype(jnp.bfloat16)
    x3 = x2 + _dot(act_ref[...], wdown_ref[...])
    o_ref[0] = _rmsnorm(x3, gfin_ref[...])


def _const_spec(shape):
    return pl.BlockSpec(shape, lambda *_: (0,) * len(shape), pipeline_mode=pl.Buffered(1))


def _row(a):
    return a.reshape(1, -1).astype(jnp.float32)


def _tile_spec():
    return pl.BlockSpec((1, SEQ_TILE, D_MODEL), lambda b, s: (b, s, 0))


def _seq_params():
    return pltpu.CompilerParams(dimension_semantics=("arbitrary", "arbitrary"),
                                vmem_limit_bytes=VMEM_LIMIT)


def _mem_kv_call(mem, norm_mem_g, wkv):
    B = mem.shape[0]
    bf16 = jnp.bfloat16
    return pl.pallas_call(
        _memkv_kernel,
        grid=(B,),
        in_specs=[pl.BlockSpec((1, MEM_LEN, D_MODEL), lambda b: (b, 0, 0)),
                  _const_spec((1, D_MODEL)), _const_spec((D_MODEL, 2 * D_MODEL))],
        out_specs=[pl.BlockSpec((1, MEM_LEN, D_MODEL), lambda b: (b, 0, 0))] * 2,
        out_shape=[jax.ShapeDtypeStruct((B, MEM_LEN, D_MODEL), bf16)] * 2,
        compiler_params=pltpu.CompilerParams(dimension_semantics=("arbitrary",),
                                             vmem_limit_bytes=VMEM_LIMIT),
        name="mem_kv",
    )(mem, _row(norm_mem_g), wkv.astype(bf16))


def _mixer_call(x, norm_g, w_in, b_gate_if, conv_w, conv_b, sinks, w_out):
    B, S, D = x.shape
    bf16 = jnp.bfloat16
    f32 = jnp.float32
    w_in_main = w_in[:, :IN_MAIN].astype(bf16)
    H = MLSTM_HEADS
    w_gate = jnp.zeros((D, GATE_PAD), f32)
    w_gate = w_gate.at[:, 0:H].set(w_in[:, IN_MAIN:IN_MAIN + H])
    w_gate = w_gate.at[:, 128:128 + H].set(w_in[:, IN_MAIN + H:IN_MAIN + 2 * H]).astype(bf16)
    b_gate = jnp.zeros((1, GATE_PAD), f32)
    b_gate = b_gate.at[0, 0:H].set(b_gate_if[0:H].astype(f32))
    b_gate = b_gate.at[0, 128:128 + H].set(b_gate_if[H:2 * H].astype(f32))
    slopes = tuple(float(2.0 ** (-8.0 * (i + 1) / ATT_HEADS)) for i in range(ATT_HEADS))
    mix_spec = pl.BlockSpec((MIX_SEQS, MIX_TILE, D), lambda b, s: (b, s, 0))
    return pl.pallas_call(
        functools.partial(_mixer_kernel, slopes=slopes),
        grid=(B // MIX_SEQS, S // MIX_TILE),
        in_specs=[pl.BlockSpec(memory_space=pltpu.SMEM),
                  mix_spec,
                  _const_spec((1, D)),
                  _const_spec((D, IN_MAIN)),
                  _const_spec((D, GATE_PAD)),
                  _const_spec((1, GATE_PAD)),
                  _const_spec((QK_CONV_WIDTH, 2 * MLSTM_WIDTH)),
                  _const_spec((1, 2 * MLSTM_WIDTH)),
                  _const_spec((D, D))],
        out_specs=mix_spec,
        out_shape=jax.ShapeDtypeStruct((B, S, D), f32),
        scratch_shapes=[
            pltpu.VMEM((MIX_SEQS, 2 * SWA_TILES, WINDOW + MIX_TILE, 128), bf16),
            pltpu.VMEM((MIX_SEQS, CARRY_ROWS + MIX_TILE, 2 * MLSTM_WIDTH), f32),
            pltpu.VMEM((MIX_SEQS, MLSTM_HEADS, MLSTM_HEAD_DIM, 2 * MLSTM_HEAD_DIM), f32),
            pltpu.VMEM((MIX_SEQS, CARRY_ROWS, 128), f32),
            pltpu.VMEM((ATT_HEADS, WINDOW, 2 * WINDOW), f32),
            pltpu.VMEM((ATT_HEADS, WINDOW, 2 * WINDOW), f32),
        ] + [
            pltpu.VMEM((MIX_ROWS, ATT_WIDTH), bf16),
            pltpu.VMEM((MIX_ROWS, 2 * MLSTM_WIDTH), f32),
            pltpu.VMEM((MIX_ROWS, MLSTM_WIDTH), bf16),
            pltpu.VMEM((MIX_ROWS, MLSTM_WIDTH), f32),
            pltpu.VMEM((MIX_ROWS, GATE_PAD), f32),
        ] * (MIX_SEQS * MIX_TILE // MIX_ROWS),
        compiler_params=_seq_params(),
        name="token_mixer",
    )(sinks.astype(f32), x, _row(norm_g), w_in_main, w_gate, b_gate,
      conv_w.astype(f32), _row(conv_b), w_out.astype(bf16))


def _ffn_call(x1, mem_k, mem_v, norm_x_g, wq, wo, norm_f_g, w_up, conv_w, conv_b, w_down,
              norm_final_g):
    B, S, D = x1.shape
    bf16 = jnp.bfloat16
    f32 = jnp.float32
    kv_spec = pl.BlockSpec((1, MEM_LEN, D), lambda b, s: (b, 0, 0))
    return pl.pallas_call(
        _ffn_kernel,
        grid=(B, S // SEQ_TILE),
        in_specs=[_tile_spec(), kv_spec, kv_spec,
                  _const_spec((1, D)), _const_spec((D, D)), _const_spec((D, D)),
                  _const_spec((1, D)), _const_spec((D, 2 * D_FF)),
                  _const_spec((FFN_CONV_WIDTH, D_FF)), _const_spec((1, D_FF)),
                  _const_spec((D_FF, D)), _const_spec((1, D))],
        out_specs=_tile_spec(),
        out_shape=jax.ShapeDtypeStruct((B, S, D), f32),
        scratch_shapes=[
            pltpu.VMEM((SEQ_TILE, D), bf16),
            pltpu.VMEM((CARRY_ROWS + SEQ_TILE, D_FF), f32),
            pltpu.VMEM((SEQ_TILE, D_FF), bf16),
        ],
        compiler_params=_seq_params(),
        name="xattn_ffn",
    )(x1, mem_k, mem_v, _row(norm_x_g), wq.astype(bf16), wo.astype(bf16), _row(norm_f_g),
      w_up.astype(bf16), conv_w.astype(f32), _row(conv_b), w_down.astype(bf16),
      _row(norm_final_g))


def kernel(x, mem, norm_mix_g, w_in, b_gate_if, conv_qk_w, conv_qk_b, attn_sinks, w_out,
           norm_xattn_g, norm_mem_g, wq_x, wkv_x, wo_x, norm_ffn_g, w_up, conv_ffn_w,
           conv_ffn_b, w_down, norm_final_g):
    assert x.shape[2] == D_MODEL and x.shape[1] % SEQ_TILE == 0 and x.shape[1] % MIX_TILE == 0 and w_in.shape[0] == 1
    mem_k, mem_v = _mem_kv_call(mem, norm_mem_g[0], wkv_x[0])
    x1 = _mixer_call(x, norm_mix_g[0], w_in[0], b_gate_if[0], conv_qk_w[0], conv_qk_b[0],
                     attn_sinks[0], w_out[0])
    return _ffn_call(x1, mem_k, mem_v, norm_xattn_g[0], wq_x[0], wo_x[0], norm_ffn_g[0],
                     w_up[0], conv_ffn_w[0], conv_ffn_b[0], w_down[0], norm_final_g)
```

```python
import functools
import itertools

import jax
import jax.numpy as jnp
from jax import lax
from jax.experimental import pallas as pl
from jax.experimental.pallas import tpu as pltpu

D_MODEL = 1024
ATT_HEADS = 8
ATT_KV_HEADS = 2
ATT_GROUP = ATT_HEADS // ATT_KV_HEADS
ATT_HEAD_DIM = 64
ATT_WIDTH = ATT_HEADS * ATT_HEAD_DIM
ATT_KV_WIDTH = ATT_KV_HEADS * ATT_HEAD_DIM
WINDOW = 128
MLSTM_HEADS = 4
MLSTM_WIDTH = 512
MLSTM_HEAD_DIM = 128
QK_CONV_WIDTH = 4
X_HEADS = 4
X_HEAD_DIM = 256
MEM_LEN = 256
D_FF = 2816
FFN_CONV_WIDTH = 3
RMS_EPS = 1e-6
NEG_BIG = -1e30

O_AQ = 0
O_AK = ATT_WIDTH
O_AV = O_AK + ATT_KV_WIDTH
O_MQK = O_AV + ATT_KV_WIDTH
O_MV = O_MQK + 2 * MLSTM_WIDTH
O_MO = O_MV + MLSTM_WIDTH
IN_MAIN = O_MO + MLSTM_WIDTH
GATE_PAD = 256

SEQ_TILE = 512
MIX_TILE = 1024
N_STEP_REFS = 6
MIX_ROWS = WINDOW
PROJ_COLS = 256
OUT_COLS = 512
FFN_COLS = 256
SWA_TILES = 4
CARRY_ROWS = 8
VMEM_LIMIT = 56 * 1024 * 1024


def _rmsnorm(x, g):
    return x * lax.rsqrt(jnp.mean(x * x, axis=-1, keepdims=True) + RMS_EPS) * g


def _dot(a, b):
    return jnp.dot(a, b, preferred_element_type=jnp.float32)


def _dot_nt(a, b):
    return lax.dot_general(a, b, (((1,), (1,)), ((), ())),
                           preferred_element_type=jnp.float32)


def _log_sigmoid(x):
    return jnp.minimum(x, 0.0) - jnp.log1p(jnp.exp(-jnp.abs(x)))


def _sigmoid(x):
    return 1.0 / (1.0 + jnp.exp(-x))


def _memkv_kernel(mem_ref, g_ref, wkv_ref, k_ref, v_ref):
    hm = _rmsnorm(mem_ref[0], g_ref[...]).astype(jnp.bfloat16)
    kv = _dot(hm, wkv_ref[...])
    k_ref[0] = kv[:, :D_MODEL].astype(jnp.bfloat16)
    v_ref[0] = kv[:, D_MODEL:].astype(jnp.bfloat16)


def _band_bias(slopes, first):
    W = WINDOW
    qi = lax.broadcasted_iota(jnp.int32, (W, 2 * W), 0)
    kj = lax.broadcasted_iota(jnp.int32, (W, 2 * W), 1)
    dist = qi + W - kj
    valid = (dist >= 0) & (dist < W)
    if first:
        valid = valid & (kj >= W)
    distf = dist.astype(jnp.float32)
    return [jnp.where(valid, -(sl * distf), NEG_BIG) for sl in slopes]


def _cumsum_rows(tril, v, lane):
    n = MLSTM_HEADS
    hi = v.astype(jnp.bfloat16)
    r1 = v - hi.astype(jnp.float32)
    mid = r1.astype(jnp.bfloat16)
    lo = (r1 - mid.astype(jnp.float32)).astype(jnp.bfloat16)
    packed = jnp.where(lane < n, hi.astype(jnp.float32),
                       jnp.where(lane < 2 * n, pltpu.roll(mid.astype(jnp.float32), n, axis=1),
                                 jnp.where(lane < 3 * n, pltpu.roll(lo.astype(jnp.float32), 2 * n, axis=1),
                                           0.0)))
    r = _dot(tril, packed.astype(jnp.bfloat16))
    return r + pltpu.roll(r, 128 - n, axis=1) + pltpu.roll(r, 128 - 2 * n, axis=1)


def _alternate(*stage_iters):
    live = list(stage_iters)
    while live:
        live = [it for it in live if next(it, _DONE) is not _DONE]


_DONE = object()


def _mixer_kernel(sinks_ref, x_ref, g_ref, win_ref, wgate_ref, bgate_ref, convw_ref,
                  convb_ref, wout_ref, o_ref,
                  kv_buf_ref, qk_buf_ref, c_ref, m_ref, bias_ref, bias0_ref,
                  *step_refs, slopes):
    s_idx = pl.program_id(1)
    x_ref = x_ref.at[0]
    o_ref = o_ref.at[0]
    ts = MIX_TILE
    W = WINDOW
    L = MIX_ROWS
    bf16 = jnp.bfloat16

    @pl.when(s_idx == 0)
    def _():
        kv_buf_ref[:, 0:W, :] = jnp.zeros((2 * SWA_TILES, W, 128), bf16)
        qk_buf_ref[0:CARRY_ROWS, :] = jnp.zeros((CARRY_ROWS, 2 * MLSTM_WIDTH), jnp.float32)
        c_ref[...] = jnp.zeros_like(c_ref)
        m_ref[...] = jnp.zeros_like(m_ref)
        for head, (bias, bias0) in enumerate(zip(_band_bias(slopes, False), _band_bias(slopes, True))):
            bias_ref[head] = bias
            bias0_ref[head] = bias0

    @pl.when(s_idx == 1)
    def _():
        bias0_ref[...] = bias_ref[...]

    @pl.when(s_idx > 0)
    def _():
        kv_buf_ref[:, 0:W, :] = kv_buf_ref[:, ts:ts + W, :]
        qk_buf_ref[0:CARRY_ROWS, :] = qk_buf_ref[ts:ts + CARRY_ROWS, :]

    lane = lax.broadcasted_iota(jnp.int32, (L, 128), 1)
    low = lane < ATT_HEAD_DIM
    zero_tile = jnp.zeros((L, 128), bf16)
    row = lax.broadcasted_iota(jnp.int32, (L, L), 0)
    col = lax.broadcasted_iota(jnp.int32, (L, L), 1)
    causal = row >= col
    tril = causal.astype(bf16)
    ones_col = (lane == 0).astype(bf16)

    def project(blk):
        r0 = blk * L
        q_ref, mqk_ref, vb_ref, og_ref, gates_ref, _ = step_refs[N_STEP_REFS * blk:N_STEP_REFS * blk + N_STEP_REFS]
        x = x_ref[r0:r0 + L, :]
        h = _rmsnorm(x, g_ref[...]).astype(bf16)

        def cols(c0, width):
            return _dot(h, win_ref[:, c0:c0 + width])

        def conv_qk(c0):
            qk_buf_ref[CARRY_ROWS + r0:CARRY_ROWS + r0 + L, c0:c0 + PROJ_COLS] = cols(O_MQK + c0, PROJ_COLS)
            conv = convb_ref[:, c0:c0 + PROJ_COLS]
            for j in range(QK_CONV_WIDTH):
                start = CARRY_ROWS + r0 - (QK_CONV_WIDTH - 1 - j)
                conv = conv + (convw_ref[j:j + 1, c0:c0 + PROJ_COLS]
                               * qk_buf_ref[start:start + L, c0:c0 + PROJ_COLS])
            mqk_ref[:, c0:c0 + PROJ_COLS] = conv * _sigmoid(conv)

        def attn_kv():
            kv = cols(O_AK, 2 * ATT_KV_WIDTH)
            for which in range(2):
                t32 = kv[:, which * ATT_KV_WIDTH:(which + 1) * ATT_KV_WIDTH]
                tb = t32.astype(bf16)
                rb = pltpu.roll(t32, ATT_HEAD_DIM, axis=1).astype(bf16)
                base = which * SWA_TILES
                kv_buf_ref[base + 0, W + r0:W + r0 + L, :] = jnp.where(low, tb, zero_tile)
                kv_buf_ref[base + 1, W + r0:W + r0 + L, :] = jnp.where(low, zero_tile, rb)
                kv_buf_ref[base + 2, W + r0:W + r0 + L, :] = jnp.where(low, rb, zero_tile)
                kv_buf_ref[base + 3, W + r0:W + r0 + L, :] = jnp.where(low, zero_tile, tb)

        def attn_q():
            q_ref[...] = (cols(O_AQ, ATT_WIDTH) * (ATT_HEAD_DIM ** -0.5)).astype(bf16)

        def mlstm_v():
            vb_ref[...] = cols(O_MV, MLSTM_WIDTH).astype(bf16)

        def mlstm_o():
            og_ref[...] = _sigmoid(cols(O_MO, MLSTM_WIDTH))

        def gates():
            gates_ref[...] = _dot(h, wgate_ref[...]) + bgate_ref[...]

        conv_groups = [functools.partial(conv_qk, c0) for c0 in range(0, 2 * MLSTM_WIDTH, PROJ_COLS)]
        for stage in conv_groups + [attn_kv, attn_q, mlstm_v, mlstm_o, gates]:
            stage()
            yield

    def attend(blk):
        r0 = blk * L
        q_ref, _, _, _, _, cat_ref = step_refs[N_STEP_REFS * blk:N_STEP_REFS * blk + N_STEP_REFS]

        bref = bias0_ref if blk == 0 else bias_ref
        scores = []
        for hk in range(ATT_KV_HEADS):
            qq = jnp.concatenate([q_ref[:, hk * 256:hk * 256 + 128],
                                  q_ref[:, hk * 256 + 128:hk * 256 + 256]], axis=0)
            scores.append([_dot_nt(qq, kv_buf_ref[2 * hk + par, r0:r0 + 2 * W, :]) for par in range(2)])
        yield
        for hk in range(ATT_KV_HEADS):
            e_rows = [[], []]
            scale_rows = []
            for half in range(2):
                rden = []
                for par in range(2):
                    head = ATT_GROUP * hk + 2 * half + par
                    sc = scores[hk][par][half * W:(half + 1) * W] + bref[head]
                    sink = sinks_ref[head]
                    mx = jnp.maximum(jnp.max(sc, axis=-1, keepdims=True), sink)
                    e = jnp.exp(sc - mx)
                    den = jnp.sum(e, axis=-1, keepdims=True) + jnp.exp(sink - mx)
                    e_rows[par].append(e.astype(bf16))
                    rden.append(1.0 / den)
                scale_rows.append(jnp.where(low, rden[0], rden[1]))
                if half == 0:
                    yield
            e_all = jnp.concatenate([jnp.concatenate(e_rows[0], axis=0),
                                     jnp.concatenate(e_rows[1], axis=0)], axis=1)
            vz = jnp.concatenate([kv_buf_ref[SWA_TILES + 2 * hk, r0:r0 + 2 * W, :],
                                  kv_buf_ref[SWA_TILES + 2 * hk + 1, r0:r0 + 2 * W, :]], axis=0)
            o = (_dot(e_all, vz) * jnp.concatenate(scale_rows, axis=0)).astype(bf16)
            cat_ref[:, hk * 256:hk * 256 + 128] = o[0:W]
            cat_ref[:, hk * 256 + 128:hk * 256 + 256] = o[W:2 * W]
            yield

    def recur(blk):
        _, mqk_ref, vb_ref, og_ref, gates_ref, cat_ref = step_refs[N_STEP_REFS * blk:N_STEP_REFS * blk + N_STEP_REFS]

        ig = gates_ref[:, 0:128]
        lf = _log_sigmoid(gates_ref[:, 128:256])
        b = _cumsum_rows(tril, lf, lane)
        a = ig - b
        a_t = a.T
        m_prev = m_ref[0:1, :]
        dmats = []
        rowmax = None
        for hd in range(MLSTM_HEADS):
            dm = jnp.where(causal, b[:, hd:hd + 1] + a_t[hd:hd + 1, :], NEG_BIG)
            dmats.append(dm)
            rm = jnp.max(dm, axis=-1, keepdims=True)
            rowmax = jnp.broadcast_to(rm, (L, 128)) if hd == 0 else jnp.where(lane == hd, rm, rowmax)
        inter = b + m_prev
        m_t = jnp.maximum(inter, rowmax)
        w_inter = jnp.exp(inter - m_t)
        e_negm = jnp.exp(-m_t)
        b_end = b[L - 1:L, :]
        g = b_end + a
        m_new = jnp.maximum(b_end + m_prev, jnp.max(g, axis=0, keepdims=True))
        decay = jnp.exp(b_end + m_prev - m_new)
        ws = jnp.exp(g - m_new)
        m_ref[0:1, :] = m_new
        yield
        for hd in range(MLSTM_HEADS):
            c0 = hd * MLSTM_HEAD_DIM
            dexp = jnp.exp(dmats[hd] - m_t[:, hd:hd + 1])
            qf = mqk_ref[:, c0:c0 + MLSTM_HEAD_DIM] * (MLSTM_HEAD_DIM ** -0.5)
            kf = mqk_ref[:, MLSTM_WIDTH + c0:MLSTM_WIDTH + c0 + MLSTM_HEAD_DIM]
            qb = qf.astype(bf16)
            kb = kf.astype(bf16)
            vb = vb_ref[:, c0:c0 + MLSTM_HEAD_DIM]
            v_ext = jnp.concatenate([vb, ones_col], axis=1)
            sc = _dot_nt(qb, kb) * dexp
            c_old = c_ref[hd]
            qw = (qf * w_inter[:, hd:hd + 1]).astype(bf16)
            num_ext = _dot(jnp.concatenate([qw, sc.astype(bf16)], axis=1),
                           jnp.concatenate([c_old.astype(bf16), v_ext], axis=0))
            num = num_ext[:, :MLSTM_HEAD_DIM]
            den = num_ext[:, MLSTM_HEAD_DIM:MLSTM_HEAD_DIM + 1]
            hh = num * (1.0 / jnp.maximum(jnp.abs(den), e_negm[:, hd:hd + 1]))
            og = og_ref[:, c0:c0 + MLSTM_HEAD_DIM]
            cat_ref[:, ATT_WIDTH + c0:ATT_WIDTH + c0 + MLSTM_HEAD_DIM] = (og * hh).astype(bf16)
            kwt = (kf * ws[:, hd:hd + 1]).T.astype(bf16)
            c_ref[hd] = decay[0:1, hd:hd + 1] * c_old + _dot(kwt, v_ext)
            yield

    def emit(blk):
        r0 = blk * L
        x = x_ref[r0:r0 + L, :]
        cat = step_refs[N_STEP_REFS * blk + N_STEP_REFS - 1][...]
        for c0 in range(0, D_MODEL, OUT_COLS):
            o_ref[r0:r0 + L, c0:c0 + OUT_COLS] = x[:, c0:c0 + OUT_COLS] + _dot(cat, wout_ref[:, c0:c0 + OUT_COLS])
            yield

    n_blk = ts // L
    stages = lambda fn, blk: fn(blk) if 0 <= blk < n_blk else iter(())
    _alternate(project(0))
    for blk in range(n_blk):
        _alternate(itertools.chain(attend(blk), recur(blk), emit(blk)), stages(project, blk + 1))


def _ffn_kernel(x_ref, k_ref, v_ref, gx_ref, wq_ref, wo_ref, gf_ref, wup_ref, convw_ref,
                convb_ref, wdown_ref, gfin_ref, o_ref, cat_ref, u_buf_ref, act_ref):
    s_idx = pl.program_id(1)
    ts = SEQ_TILE

    @pl.when(s_idx == 0)
    def _():
        u_buf_ref[0:CARRY_ROWS, :] = jnp.zeros((CARRY_ROWS, D_FF), jnp.float32)

    @pl.when(s_idx > 0)
    def _():
        u_buf_ref[0:CARRY_ROWS, :] = u_buf_ref[ts:ts + CARRY_ROWS, :]

    x1 = x_ref[0]
    hq = _rmsnorm(x1, gx_ref[...]).astype(jnp.bfloat16)
    q = (_dot(hq, wq_ref[...]) * (X_HEAD_DIM ** -0.5)).astype(jnp.bfloat16)
    for hd in range(X_HEADS):
        c0 = hd * X_HEAD_DIM
        sc = _dot_nt(q[:, c0:c0 + X_HEAD_DIM], k_ref[0, :, c0:c0 + X_HEAD_DIM])
        e = jnp.exp(sc - jnp.max(sc, axis=-1, keepdims=True))
        den = jnp.sum(e, axis=-1, keepdims=True)
        o = _dot(e.astype(jnp.bfloat16), v_ref[0, :, c0:c0 + X_HEAD_DIM]) / den
        cat_ref[:, c0:c0 + X_HEAD_DIM] = o.astype(jnp.bfloat16)
    x2 = x1 + _dot(cat_ref[...], wo_ref[...])

    hf = _rmsnorm(x2, gf_ref[...]).astype(jnp.bfloat16)
    for j in range(D_FF // FFN_COLS):
        c0 = j * FFN_COLS
        u_buf_ref[CARRY_ROWS:CARRY_ROWS + ts, c0:c0 + FFN_COLS] = _dot(hf, wup_ref[:, c0:c0 + FFN_COLS])
        up = _dot(hf, wup_ref[:, D_FF + c0:D_FF + c0 + FFN_COLS])
        g = convb_ref[:, c0:c0 + FFN_COLS]
        for t in range(FFN_CONV_WIDTH):
            shift = FFN_CONV_WIDTH - 1 - t
            g = g + (convw_ref[t:t + 1, c0:c0 + FFN_COLS]
                     * u_buf_ref[CARRY_ROWS - shift:CARRY_ROWS - shift + ts, c0:c0 + FFN_COLS])
        act_ref[:, c0:c0 + FFN_COLS] = (g * _sigmoid(g) * up).astype(jnp.bfloat16)
    x3 = x2 + _dot(act_ref[...], wdown_ref[...])
    o_ref[0] = _rmsnorm(x3, gfin_ref[...])


def _const_spec(shape):
    return pl.BlockSpec(shape, lambda *_: (0,) * len(shape), pipeline_mode=pl.Buffered(1))


def _row(a):
    return a.reshape(1, -1).astype(jnp.float32)


def _tile_spec():
    return pl.BlockSpec((1, SEQ_TILE, D_MODEL), lambda b, s: (b, s, 0))


def _seq_params():
    return pltpu.CompilerParams(dimension_semantics=("arbitrary", "arbitrary"),
                                vmem_limit_bytes=VMEM_LIMIT)


def _mem_kv_call(mem, norm_mem_g, wkv):
    B = mem.shape[0]
    bf16 = jnp.bfloat16
    return pl.pallas_call(
        _memkv_kernel,
        grid=(B,),
        in_specs=[pl.BlockSpec((1, MEM_LEN, D_MODEL), lambda b: (b, 0, 0)),
                  _const_spec((1, D_MODEL)), _const_spec((D_MODEL, 2 * D_MODEL))],
        out_specs=[pl.BlockSpec((1, MEM_LEN, D_MODEL), lambda b: (b, 0, 0))] * 2,
        out_shape=[jax.ShapeDtypeStruct((B, MEM_LEN, D_MODEL), bf16)] * 2,
        compiler_params=pltpu.CompilerParams(dimension_semantics=("arbitrary",),
                                             vmem_limit_bytes=VMEM_LIMIT),
        name="mem_kv",
    )(mem, _row(norm_mem_g), wkv.astype(bf16))


def _mixer_call(x, norm_g, w_in, b_gate_if, conv_w, conv_b, sinks, w_out):
    B, S, D = x.shape
    bf16 = jnp.bfloat16
    f32 = jnp.float32
    w_in_main = w_in[:, :IN_MAIN].astype(bf16)
    H = MLSTM_HEADS
    w_gate = jnp.zeros((D, GATE_PAD), f32)
    w_gate = w_gate.at[:, 0:H].set(w_in[:, IN_MAIN:IN_MAIN + H])
    w_gate = w_gate.at[:, 128:128 + H].set(w_in[:, IN_MAIN + H:IN_MAIN + 2 * H]).astype(bf16)
    b_gate = jnp.zeros((1, GATE_PAD), f32)
    b_gate = b_gate.at[0, 0:H].set(b_gate_if[0:H].astype(f32))
    b_gate = b_gate.at[0, 128:128 + H].set(b_gate_if[H:2 * H].astype(f32))
    slopes = tuple(float(2.0 ** (-8.0 * (i + 1) / ATT_HEADS)) for i in range(ATT_HEADS))
    mix_spec = pl.BlockSpec((1, MIX_TILE, D), lambda b, s: (b, s, 0))
    return pl.pallas_call(
        functools.partial(_mixer_kernel, slopes=slopes),
        grid=(B, S // MIX_TILE),
        in_specs=[pl.BlockSpec(memory_space=pltpu.SMEM),
                  mix_spec,
                  _const_spec((1, D)),
                  _const_spec((D, IN_MAIN)),
                  _const_spec((D, GATE_PAD)),
                  _const_spec((1, GATE_PAD)),
                  _const_spec((QK_CONV_WIDTH, 2 * MLSTM_WIDTH)),
                  _const_spec((1, 2 * MLSTM_WIDTH)),
                  _const_spec((D, D))],
        out_specs=mix_spec,
        out_shape=jax.ShapeDtypeStruct((B, S, D), f32),
        scratch_shapes=[
            pltpu.VMEM((2 * SWA_TILES, WINDOW + MIX_TILE, 128), bf16),
            pltpu.VMEM((CARRY_ROWS + MIX_TILE, 2 * MLSTM_WIDTH), f32),
            pltpu.VMEM((MLSTM_HEADS, MLSTM_HEAD_DIM, 2 * MLSTM_HEAD_DIM), f32),
            pltpu.VMEM((CARRY_ROWS, 128), f32),
            pltpu.VMEM((ATT_HEADS, WINDOW, 2 * WINDOW), f32),
            pltpu.VMEM((ATT_HEADS, WINDOW, 2 * WINDOW), f32),
        ] + [
            pltpu.VMEM((MIX_ROWS, ATT_WIDTH), bf16),
            pltpu.VMEM((MIX_ROWS, 2 * MLSTM_WIDTH), f32),
            pltpu.VMEM((MIX_ROWS, MLSTM_WIDTH), bf16),
            pltpu.VMEM((MIX_ROWS, MLSTM_WIDTH), f32),
            pltpu.VMEM((MIX_ROWS, GATE_PAD), f32),
            pltpu.VMEM((MIX_ROWS, D), bf16),
        ] * (MIX_TILE // MIX_ROWS),
        compiler_params=_seq_params(),
        name="token_mixer",
    )(sinks.astype(f32), x, _row(norm_g), w_in_main, w_gate, b_gate,
      conv_w.astype(f32), _row(conv_b), w_out.astype(bf16))


def _ffn_call(x1, mem_k, mem_v, norm_x_g, wq, wo, norm_f_g, w_up, conv_w, conv_b, w_down,
              norm_final_g):
    B, S, D = x1.shape
    bf16 = jnp.bfloat16
    f32 = jnp.float32
    kv_spec = pl.BlockSpec((1, MEM_LEN, D), lambda b, s: (b, 0, 0))
    return pl.pallas_call(
        _ffn_kernel,
        grid=(B, S // SEQ_TILE),
        in_specs=[_tile_spec(), kv_spec, kv_spec,
                  _const_spec((1, D)), _const_spec((D, D)), _const_spec((D, D)),
                  _const_spec((1, D)), _const_spec((D, 2 * D_FF)),
                  _const_spec((FFN_CONV_WIDTH, D_FF)), _const_spec((1, D_FF)),
                  _const_spec((D_FF, D)), _const_spec((1, D))],
        out_specs=_tile_spec(),
        out_shape=jax.ShapeDtypeStruct((B, S, D), f32),
        scratch_shapes=[
            pltpu.VMEM((SEQ_TILE, D), bf16),
            pltpu.VMEM((CARRY_ROWS + SEQ_TILE, D_FF), f32),
            pltpu.VMEM((SEQ_TILE, D_FF), bf16),
        ],
        compiler_params=_seq_params(),
        name="xattn_ffn",
    )(x1, mem_k, mem_v, _row(norm_x_g), wq.astype(bf16), wo.astype(bf16), _row(norm_f_g),
      w_up.astype(bf16), conv_w.astype(f32), _row(conv_b), w_down.astype(bf16),
      _row(norm_final_g))


def kernel(x, mem, norm_mix_g, w_in, b_gate_if, conv_qk_w, conv_qk_b, attn_sinks, w_out,
           norm_xattn_g, norm_mem_g, wq_x, wkv_x, wo_x, norm_ffn_g, w_up, conv_ffn_w,
           conv_ffn_b, w_down, norm_final_g):
    assert x.shape[2] == D_MODEL and x.shape[1] % SEQ_TILE == 0 and x.shape[1] % MIX_TILE == 0 and w_in.shape[0] == 1
    mem_k, mem_v = _mem_kv_call(mem, norm_mem_g[0], wkv_x[0])
    x1 = _mixer_call(x, norm_mix_g[0], w_in[0], b_gate_if[0], conv_qk_w[0], conv_qk_b[0],
                     attn_sinks[0], w_out[0])
    return _ffn_call(x1, mem_k, mem_v, norm_xattn_g[0], wq_x[0], wo_x[0], norm_ffn_g[0],
                     w_up[0], conv_ffn_w[0], conv_ffn_b[0], w_down[0], norm_final_g)
```

```python
import functools

import jax
import jax.numpy as jnp
from jax import lax
from jax.experimental import pallas as pl
from jax.experimental.pallas import tpu as pltpu

D_MODEL = 1024
ATT_HEADS = 8
ATT_KV_HEADS = 2
ATT_GROUP = ATT_HEADS // ATT_KV_HEADS
ATT_HEAD_DIM = 64
ATT_WIDTH = ATT_HEADS * ATT_HEAD_DIM
ATT_KV_WIDTH = ATT_KV_HEADS * ATT_HEAD_DIM
WINDOW = 128
MLSTM_HEADS = 4
MLSTM_WIDTH = 512
MLSTM_HEAD_DIM = 128
QK_CONV_WIDTH = 4
X_HEADS = 4
X_HEAD_DIM = 256
MEM_LEN = 256
D_FF = 2816
FFN_CONV_WIDTH = 3
RMS_EPS = 1e-6
NEG_BIG = -1e30

O_AQ = 0
O_AK = ATT_WIDTH
O_AV = O_AK + ATT_KV_WIDTH
O_MQK = O_AV + ATT_KV_WIDTH
O_MV = O_MQK + 2 * MLSTM_WIDTH
O_MO = O_MV + MLSTM_WIDTH
IN_MAIN = O_MO + MLSTM_WIDTH
GATE_PAD = 256

SEQ_TILE = 512
MIX_TILE = 1024
MIX_SEQS = 1
MIX_ROWS = WINDOW
PROJ_COLS = 256
OUT_COLS = 512
FFN_COLS = 256
SWA_TILES = 4
SWA_ROWS = 32
CARRY_ROWS = 8
VMEM_LIMIT = 56 * 1024 * 1024


def _rmsnorm(x, g):
    return x * lax.rsqrt(jnp.mean(x * x, axis=-1, keepdims=True) + RMS_EPS) * g


def _dot(a, b):
    return jnp.dot(a, b, preferred_element_type=jnp.float32)


def _dot_nt(a, b):
    return lax.dot_general(a, b, (((1,), (1,)), ((), ())),
                           preferred_element_type=jnp.float32)


def _log_sigmoid(x):
    return jnp.minimum(x, 0.0) - jnp.log1p(jnp.exp(-jnp.abs(x)))


def _sigmoid(x):
    return 1.0 / (1.0 + jnp.exp(-x))


def _memkv_kernel(mem_ref, g_ref, wkv_ref, k_ref, v_ref):
    hm = _rmsnorm(mem_ref[0], g_ref[...]).astype(jnp.bfloat16)
    kv = _dot(hm, wkv_ref[...])
    k_ref[0] = kv[:, :D_MODEL].astype(jnp.bfloat16)
    v_ref[0] = kv[:, D_MODEL:].astype(jnp.bfloat16)


def _band_bias(slopes, first):
    W = WINDOW
    qi = lax.broadcasted_iota(jnp.int32, (W, 2 * W), 0)
    kj = lax.broadcasted_iota(jnp.int32, (W, 2 * W), 1)
    dist = qi + W - kj
    valid = (dist >= 0) & (dist < W)
    if first:
        valid = valid & (kj >= W)
    distf = dist.astype(jnp.float32)
    return [jnp.where(valid, -(sl * distf), NEG_BIG) for sl in slopes]


def _cumsum_rows(tril, v, lane):
    n = MLSTM_HEADS
    hi = v.astype(jnp.bfloat16)
    r1 = v - hi.astype(jnp.float32)
    mid = r1.astype(jnp.bfloat16)
    lo = (r1 - mid.astype(jnp.float32)).astype(jnp.bfloat16)
    packed = jnp.where(lane < n, hi.astype(jnp.float32),
                       jnp.where(lane < 2 * n, pltpu.roll(mid.astype(jnp.float32), n, axis=1),
                                 jnp.where(lane < 3 * n, pltpu.roll(lo.astype(jnp.float32), 2 * n, axis=1),
                                           0.0)))
    r = _dot(tril, packed.astype(jnp.bfloat16))
    return r + pltpu.roll(r, 128 - n, axis=1) + pltpu.roll(r, 128 - 2 * n, axis=1)


def _alternate(*stage_iters):
    live = list(stage_iters)
    while live:
        live = [it for it in live if next(it, _DONE) is not _DONE]


_DONE = object()


def _mixer_kernel(sinks_ref, x_ref, g_ref, win_ref, wgate_ref, bgate_ref, convw_ref,
                  convb_ref, wout_ref, o_ref,
                  kv_buf_ref, qk_buf_ref, c_ref, m_ref, bias_ref, bias0_ref,
                  *step_refs, slopes):
    n = len(step_refs) // MIX_SEQS
    for bi in range(MIX_SEQS):
        _mixer_sequence(pl.program_id(1), sinks_ref, x_ref.at[bi], g_ref, win_ref, wgate_ref,
                        bgate_ref, convw_ref, convb_ref, wout_ref, o_ref.at[bi],
                        kv_buf_ref.at[bi], qk_buf_ref.at[bi], c_ref.at[bi], m_ref.at[bi],
                        bias_ref, bias0_ref, step_refs[bi * n:(bi + 1) * n], slopes)


def _mixer_sequence(s_idx, sinks_ref, x_ref, g_ref, win_ref, wgate_ref, bgate_ref, convw_ref,
                    convb_ref, wout_ref, o_ref,
                    kv_buf_ref, qk_buf_ref, c_ref, m_ref, bias_ref, bias0_ref,
                    step_refs, slopes):
    ts = MIX_TILE
    W = WINDOW
    L = MIX_ROWS
    bf16 = jnp.bfloat16

    @pl.when(s_idx == 0)
    def _():
        kv_buf_ref[:, 0:W, :] = jnp.zeros((2 * SWA_TILES, W, 128), bf16)
        qk_buf_ref[0:CARRY_ROWS, :] = jnp.zeros((CARRY_ROWS, 2 * MLSTM_WIDTH), jnp.float32)
        c_ref[...] = jnp.zeros_like(c_ref)
        m_ref[...] = jnp.zeros_like(m_ref)
        for head, (bias, bias0) in enumerate(zip(_band_bias(slopes, False), _band_bias(slopes, True))):
            bias_ref[head] = bias
            bias0_ref[head] = bias0

    @pl.when(s_idx == 1)
    def _():
        bias0_ref[...] = bias_ref[...]

    @pl.when(s_idx > 0)
    def _():
        kv_buf_ref[:, 0:W, :] = kv_buf_ref[:, ts:ts + W, :]
        qk_buf_ref[0:CARRY_ROWS, :] = qk_buf_ref[ts:ts + CARRY_ROWS, :]

    lane = lax.broadcasted_iota(jnp.int32, (L, 128), 1)
    low = lane < ATT_HEAD_DIM
    zero_tile = jnp.zeros((L, 128), bf16)
    lane_rc = lax.broadcasted_iota(jnp.int32, (SWA_ROWS, 128), 1)
    row = lax.broadcasted_iota(jnp.int32, (L, L), 0)
    col = lax.broadcasted_iota(jnp.int32, (L, L), 1)
    causal = row >= col
    tril = causal.astype(bf16)
    ones_col = (lane == 0).astype(bf16)

    def project(blk):
        r0 = blk * L
        q_ref, mqk_ref, vb_ref, og_ref, gates_ref = step_refs[5 * blk:5 * blk + 5]
        x = x_ref[r0:r0 + L, :]
        h = _rmsnorm(x, g_ref[...]).astype(bf16)

        def cols(c0, width):
            return _dot(h, win_ref[:, c0:c0 + width])

        for c0 in range(0, 2 * MLSTM_WIDTH, PROJ_COLS):
            qk_buf_ref[CARRY_ROWS + r0:CARRY_ROWS + r0 + L, c0:c0 + PROJ_COLS] = cols(O_MQK + c0, PROJ_COLS)
            conv = convb_ref[:, c0:c0 + PROJ_COLS]
            for j in range(QK_CONV_WIDTH):
                start = CARRY_ROWS + r0 - (QK_CONV_WIDTH - 1 - j)
                conv = conv + (convw_ref[j:j + 1, c0:c0 + PROJ_COLS]
                               * qk_buf_ref[start:start + L, c0:c0 + PROJ_COLS])
            mqk_ref[:, c0:c0 + PROJ_COLS] = conv * _sigmoid(conv)
            yield
        kv = cols(O_AK, 2 * ATT_KV_WIDTH)
        for which in range(2):
            t32 = kv[:, which * ATT_KV_WIDTH:(which + 1) * ATT_KV_WIDTH]
            tb = t32.astype(bf16)
            rb = pltpu.roll(t32, ATT_HEAD_DIM, axis=1).astype(bf16)
            base = which * SWA_TILES
            kv_buf_ref[base + 0, W + r0:W + r0 + L, :] = jnp.where(low, tb, zero_tile)
            kv_buf_ref[base + 1, W + r0:W + r0 + L, :] = jnp.where(low, zero_tile, rb)
            kv_buf_ref[base + 2, W + r0:W + r0 + L, :] = jnp.where(low, rb, zero_tile)
            kv_buf_ref[base + 3, W + r0:W + r0 + L, :] = jnp.where(low, zero_tile, tb)
        yield
        q_ref[...] = (cols(O_AQ, ATT_WIDTH) * (ATT_HEAD_DIM ** -0.5)).astype(bf16)
        yield
        vb_ref[...] = cols(O_MV, MLSTM_WIDTH).astype(bf16)
        yield
        og_ref[...] = _sigmoid(cols(O_MO, MLSTM_WIDTH))
        yield
        gates_ref[...] = _dot(h, wgate_ref[...]) + bgate_ref[...]

    def rest(blk):
        r0 = blk * L
        x = x_ref[r0:r0 + L, :]
        q_ref, mqk_ref, vb_ref, og_ref, gates_ref = step_refs[5 * blk:5 * blk + 5]

        bref = bias0_ref if blk == 0 else bias_ref
        out_tiles = []
        scores = []
        for hk in range(ATT_KV_HEADS):
            qq = jnp.concatenate([q_ref[:, hk * 256:hk * 256 + 128],
                                  q_ref[:, hk * 256 + 128:hk * 256 + 256]], axis=0)
            scores.append([_dot_nt(qq, kv_buf_ref[2 * hk + par, r0:r0 + 2 * W, :]) for par in range(2)])
        yield
        for hk in range(ATT_KV_HEADS):
            s_par = scores[hk]
            e_rows = [[], []]
            scale_rows = []
            for half in range(2):
                for rc in range(W // SWA_ROWS):
                    rr = rc * SWA_ROWS
                    rden = []
                    for par in range(2):
                        head = ATT_GROUP * hk + 2 * half + par
                        sc = (s_par[par][half * W + rr:half * W + rr + SWA_ROWS]
                              + bref[head, rr:rr + SWA_ROWS, :])
                        sink = sinks_ref[head]
                        mx = jnp.maximum(jnp.max(sc, axis=-1, keepdims=True), sink)
                        e = jnp.exp(sc - mx)
                        den = jnp.sum(e, axis=-1, keepdims=True) + jnp.exp(sink - mx)
                        e_rows[par].append(e.astype(bf16))
                        rden.append(1.0 / den)
                    scale_rows.append(jnp.where(lane_rc < ATT_HEAD_DIM, rden[0], rden[1]))
                if half == 0:
                    yield
            e_all = jnp.concatenate([jnp.concatenate(e_rows[0], axis=0),
                                     jnp.concatenate(e_rows[1], axis=0)], axis=1)
            vz = jnp.concatenate([kv_buf_ref[SWA_TILES + 2 * hk, r0:r0 + 2 * W, :],
                                  kv_buf_ref[SWA_TILES + 2 * hk + 1, r0:r0 + 2 * W, :]], axis=0)
            o = (_dot(e_all, vz) * jnp.concatenate(scale_rows, axis=0)).astype(bf16)
            out_tiles += [o[0:W], o[W:2 * W]]
            yield

        ig = gates_ref[:, 0:128]
        lf = _log_sigmoid(gates_ref[:, 128:256])
        b = _cumsum_rows(tril, lf, lane)
        a = ig - b
        a_t = a.T
        m_prev = m_ref[0:1, :]
        dmats = []
        rowmax = None
        for hd in range(MLSTM_HEADS):
            dm = jnp.where(causal, b[:, hd:hd + 1] + a_t[hd:hd + 1, :], NEG_BIG)
            dmats.append(dm)
            rm = jnp.max(dm, axis=-1, keepdims=True)
            rowmax = jnp.broadcast_to(rm, (L, 128)) if hd == 0 else jnp.where(lane == hd, rm, rowmax)
        inter = b + m_prev
        m_t = jnp.maximum(inter, rowmax)
        w_inter = jnp.exp(inter - m_t)
        e_negm = jnp.exp(-m_t)
        b_end = b[L - 1:L, :]
        g = b_end + a
        m_new = jnp.maximum(b_end + m_prev, jnp.max(g, axis=0, keepdims=True))
        decay = jnp.exp(b_end + m_prev - m_new)
        ws = jnp.exp(g - m_new)
        m_ref[0:1, :] = m_new
        yield
        for hd in range(MLSTM_HEADS):
            c0 = hd * MLSTM_HEAD_DIM
            dexp = jnp.exp(dmats[hd] - m_t[:, hd:hd + 1])
            qf = mqk_ref[:, c0:c0 + MLSTM_HEAD_DIM] * (MLSTM_HEAD_DIM ** -0.5)
            kf = mqk_ref[:, MLSTM_WIDTH + c0:MLSTM_WIDTH + c0 + MLSTM_HEAD_DIM]
            qb = qf.astype(bf16)
            kb = kf.astype(bf16)
            vb = vb_ref[:, c0:c0 + MLSTM_HEAD_DIM]
            v_ext = jnp.concatenate([vb, ones_col], axis=1)
            sc = _dot_nt(qb, kb) * dexp
            c_old = c_ref[hd]
            qw = (qf * w_inter[:, hd:hd + 1]).astype(bf16)
            num_ext = _dot(jnp.concatenate([qw, sc.astype(bf16)], axis=1),
                           jnp.concatenate([c_old.astype(bf16), v_ext], axis=0))
            num = num_ext[:, :MLSTM_HEAD_DIM]
            den = num_ext[:, MLSTM_HEAD_DIM:MLSTM_HEAD_DIM + 1]
            hh = num * (1.0 / jnp.maximum(jnp.abs(den), e_negm[:, hd:hd + 1]))
            og = og_ref[:, c0:c0 + MLSTM_HEAD_DIM]
            out_tiles.append((og * hh).astype(bf16))
            kwt = (kf * ws[:, hd:hd + 1]).T.astype(bf16)
            c_ref[hd] = decay[0:1, hd:hd + 1] * c_old + _dot(kwt, v_ext)
            yield

        cat = jnp.concatenate(out_tiles, axis=1)
        for c0 in range(0, D_MODEL, OUT_COLS):
            o_ref[r0:r0 + L, c0:c0 + OUT_COLS] = x[:, c0:c0 + OUT_COLS] + _dot(cat, wout_ref[:, c0:c0 + OUT_COLS])
            yield

    n_blk = ts // L
    _alternate(project(0))
    for blk in range(n_blk):
        _alternate(rest(blk), project(blk + 1) if blk + 1 < n_blk else iter(()))


def _ffn_kernel(x_ref, k_ref, v_ref, gx_ref, wq_ref, wo_ref, gf_ref, wup_ref, convw_ref,
                convb_ref, wdown_ref, gfin_ref, o_ref, cat_ref, u_buf_ref, act_ref):
    s_idx = pl.program_id(1)
    ts = SEQ_TILE

    @pl.when(s_idx == 0)
    def _():
        u_buf_ref[0:CARRY_ROWS, :] = jnp.zeros((CARRY_ROWS, D_FF), jnp.float32)

    @pl.when(s_idx > 0)
    def _():
        u_buf_ref[0:CARRY_ROWS, :] = u_buf_ref[ts:ts + CARRY_ROWS, :]

    x1 = x_ref[0]
    hq = _rmsnorm(x1, gx_ref[...]).astype(jnp.bfloat16)
    q = (_dot(hq, wq_ref[...]) * (X_HEAD_DIM ** -0.5)).astype(jnp.bfloat16)
    for hd in range(X_HEADS):
        c0 = hd * X_HEAD_DIM
        sc = _dot_nt(q[:, c0:c0 + X_HEAD_DIM], k_ref[0, :, c0:c0 + X_HEAD_DIM])
        e = jnp.exp(sc - jnp.max(sc, axis=-1, keepdims=True))
        den = jnp.sum(e, axis=-1, keepdims=True)
        o = _dot(e.astype(jnp.bfloat16), v_ref[0, :, c0:c0 + X_HEAD_DIM]) / den
        cat_ref[:, c0:c0 + X_HEAD_DIM] = o.astype(jnp.bfloat16)
    x2 = x1 + _dot(cat_ref[...], wo_ref[...])

    hf = _rmsnorm(x2, gf_ref[...]).astype(jnp.bfloat16)
    for j in range(D_FF // FFN_COLS):
        c0 = j * FFN_COLS
        u_buf_ref[CARRY_ROWS:CARRY_ROWS + ts, c0:c0 + FFN_COLS] = _dot(hf, wup_ref[:, c0:c0 + FFN_COLS])
        up = _dot(hf, wup_ref[:, D_FF + c0:D_FF + c0 + FFN_COLS])
        g = convb_ref[:, c0:c0 + FFN_COLS]
        for t in range(FFN_CONV_WIDTH):
            shift = FFN_CONV_WIDTH - 1 - t
            g = g + (convw_ref[t:t + 1, c0:c0 + FFN_COLS]
                     * u_buf_ref[CARRY_ROWS - shift:CARRY_ROWS - shift + ts, c0:c0 + FFN_COLS])
        act_ref[:, c0:c0 + FFN_COLS] = (g * _sigmoid(g) * up).astype(jnp.bfloat16)
    x3 = x2 + _dot(act_ref[...], wdown_ref[...])
    o_ref[0] = _rmsnorm(x3, gfin_ref[...])


def _const_spec(shape):
    return pl.BlockSpec(shape, lambda *_: (0,) * len(shape), pipeline_mode=pl.Buffered(1))


def _row(a):
    return a.reshape(1, -1).astype(jnp.float32)


def _tile_spec():
    return pl.BlockSpec((1, SEQ_TILE, D_MODEL), lambda b, s: (b, s, 0))


def _seq_params():
    return pltpu.CompilerParams(dimension_semantics=("arbitrary", "arbitrary"),
                                vmem_limit_bytes=VMEM_LIMIT)


def _mem_kv_call(mem, norm_mem_g, wkv):
    B = mem.shape[0]
    bf16 = jnp.bfloat16
    return pl.pallas_call(
        _memkv_kernel,
        grid=(B,),
        in_specs=[pl.BlockSpec((1, MEM_LEN, D_MODEL), lambda b: (b, 0, 0)),
                  _const_spec((1, D_MODEL)), _const_spec((D_MODEL, 2 * D_MODEL))],
        out_specs=[pl.BlockSpec((1, MEM_LEN, D_MODEL), lambda b: (b, 0, 0))] * 2,
        out_shape=[jax.ShapeDtypeStruct((B, MEM_LEN, D_MODEL), bf16)] * 2,
        compiler_params=pltpu.CompilerParams(dimension_semantics=("arbitrary",),
                                             vmem_limit_bytes=VMEM_LIMIT),
        name="mem_kv",
    )(mem, _row(norm_mem_g), wkv.astype(bf16))


def _mixer_call(x, norm_g, w_in, b_gate_if, conv_w, conv_b, sinks, w_out):
    B, S, D = x.shape
    bf16 = jnp.bfloat16
    f32 = jnp.float32
    w_in_main = w_in.astype(bf16)
    H = MLSTM_HEADS
    spread = lambda a: jnp.pad(a.reshape(a.shape[0], 2, H), ((0, 0), (0, 0), (0, 128 - H))).reshape(a.shape[0], GATE_PAD)
    w_gate = spread(w_in[:, IN_MAIN:IN_MAIN + 2 * H]).astype(bf16)
    b_gate = spread(b_gate_if.astype(f32).reshape(1, 2 * H))
    slopes = tuple(float(2.0 ** (-8.0 * (i + 1) / ATT_HEADS)) for i in range(ATT_HEADS))
    mix_spec = pl.BlockSpec((MIX_SEQS, MIX_TILE, D), lambda b, s: (b, s, 0))
    return pl.pallas_call(
        functools.partial(_mixer_kernel, slopes=slopes),
        grid=(B // MIX_SEQS, S // MIX_TILE),
        in_specs=[pl.BlockSpec(memory_space=pltpu.SMEM),
                  mix_spec,
                  _const_spec((1, D)),
                  _const_spec(w_in.shape),
                  _const_spec((D, GATE_PAD)),
                  _const_spec((1, GATE_PAD)),
                  _const_spec((QK_CONV_WIDTH, 2 * MLSTM_WIDTH)),
                  _const_spec((1, 2 * MLSTM_WIDTH)),
                  _const_spec((D, D))],
        out_specs=mix_spec,
        out_shape=jax.ShapeDtypeStruct((B, S, D), f32),
        scratch_shapes=[
            pltpu.VMEM((MIX_SEQS, 2 * SWA_TILES, WINDOW + MIX_TILE, 128), bf16),
            pltpu.VMEM((MIX_SEQS, CARRY_ROWS + MIX_TILE, 2 * MLSTM_WIDTH), f32),
            pltpu.VMEM((MIX_SEQS, MLSTM_HEADS, MLSTM_HEAD_DIM, 2 * MLSTM_HEAD_DIM), f32),
            pltpu.VMEM((MIX_SEQS, CARRY_ROWS, 128), f32),
            pltpu.VMEM((ATT_HEADS, WINDOW, 2 * WINDOW), f32),
            pltpu.VMEM((ATT_HEADS, WINDOW, 2 * WINDOW), f32),
        ] + [
            pltpu.VMEM((MIX_ROWS, ATT_WIDTH), bf16),
            pltpu.VMEM((MIX_ROWS, 2 * MLSTM_WIDTH), f32),
            pltpu.VMEM((MIX_ROWS, MLSTM_WIDTH), bf16),
            pltpu.VMEM((MIX_ROWS, MLSTM_WIDTH), f32),
            pltpu.VMEM((MIX_ROWS, GATE_PAD), f32),
        ] * (MIX_SEQS * MIX_TILE // MIX_ROWS),
        compiler_params=_seq_params(),
        name="token_mixer",
    )(sinks.astype(f32), x, _row(norm_g), w_in_main, w_gate, b_gate,
      conv_w.astype(f32), _row(conv_b), w_out.astype(bf16))


def _ffn_call(x1, mem_k, mem_v, norm_x_g, wq, wo, norm_f_g, w_up, conv_w, conv_b, w_down,
              norm_final_g):
    B, S, D = x1.shape
    bf16 = jnp.bfloat16
    f32 = jnp.float32
    kv_spec = pl.BlockSpec((1, MEM_LEN, D), lambda b, s: (b, 0, 0))
    return pl.pallas_call(
        _ffn_kernel,
        grid=(B, S // SEQ_TILE),
        in_specs=[_tile_spec(), kv_spec, kv_spec,
                  _const_spec((1, D)), _const_spec((D, D)), _const_spec((D, D)),
                  _const_spec((1, D)), _const_spec((D, 2 * D_FF)),
                  _const_spec((FFN_CONV_WIDTH, D_FF)), _const_spec((1, D_FF)),
                  _const_spec((D_FF, D)), _const_spec((1, D))],
        out_specs=_tile_spec(),
        out_shape=jax.ShapeDtypeStruct((B, S, D), f32),
        scratch_shapes=[
            pltpu.VMEM((SEQ_TILE, D), bf16),
            pltpu.VMEM((CARRY_ROWS + SEQ_TILE, D_FF), f32),
            pltpu.VMEM((SEQ_TILE, D_FF), bf16),
        ],
        compiler_params=_seq_params(),
        name="xattn_ffn",
    )(x1, mem_k, mem_v, _row(norm_x_g), wq.astype(bf16), wo.astype(bf16), _row(norm_f_g),
      w_up.astype(bf16), conv_w.astype(f32), _row(conv_b), w_down.astype(bf16),
      _row(norm_final_g))


def kernel(x, mem, norm_mix_g, w_in, b_gate_if, conv_qk_w, conv_qk_b, attn_sinks, w_out,
           norm_xattn_g, norm_mem_g, wq_x, wkv_x, wo_x, norm_ffn_g, w_up, conv_ffn_w,
           conv_ffn_b, w_down, norm_final_g):
    assert x.shape[2] == D_MODEL and x.shape[1] % SEQ_TILE == 0 and x.shape[1] % MIX_TILE == 0 and w_in.shape[0] == 1
    mem_k, mem_v = _mem_kv_call(mem, norm_mem_g[0], wkv_x[0])
    x1 = _mixer_call(x, norm_mix_g[0], w_in[0], b_gate_if[0], conv_qk_w[0], conv_qk_b[0],
                     attn_sinks[0], w_out[0])
    return _ffn_call(x1, mem_k, mem_v, norm_xattn_g[0], wq_x[0], wo_x[0], norm_ffn_g[0],
                     w_up[0], conv_ffn_w[0], conv_ffn_b[0], w_down[0], norm_final_g)
```

```python
import functools

import jax
import jax.numpy as jnp
from jax import lax
from jax.experimental import pallas as pl
from jax.experimental.pallas import tpu as pltpu

D_MODEL = 1024
ATT_HEADS = 8
ATT_KV_HEADS = 2
ATT_GROUP = ATT_HEADS // ATT_KV_HEADS
ATT_HEAD_DIM = 64
ATT_WIDTH = ATT_HEADS * ATT_HEAD_DIM
ATT_KV_WIDTH = ATT_KV_HEADS * ATT_HEAD_DIM
WINDOW = 128
MLSTM_HEADS = 4
MLSTM_WIDTH = 512
MLSTM_HEAD_DIM = 128
QK_CONV_WIDTH = 4
X_HEADS = 4
X_HEAD_DIM = 256
MEM_LEN = 256
D_FF = 2816
FFN_CONV_WIDTH = 3
RMS_EPS = 1e-6
NEG_BIG = -1e30

O_AQ = 0
O_AK = ATT_WIDTH
O_AV = O_AK + ATT_KV_WIDTH
O_MQK = O_AV + ATT_KV_WIDTH
O_MV = O_MQK + 2 * MLSTM_WIDTH
O_MO = O_MV + MLSTM_WIDTH
IN_MAIN = O_MO + MLSTM_WIDTH
GATE_PAD = 256

SEQ_TILE = 512
MIX_TILE = 1024
MIX_SEQS = 1
MIX_ROWS = WINDOW
PROJ_COLS = 256
OUT_COLS = 512
FFN_COLS = 256
SWA_TILES = 4
SWA_ROWS = 32
CARRY_ROWS = 8
VMEM_LIMIT = 56 * 1024 * 1024


def _rmsnorm(x, g):
    return x * lax.rsqrt(jnp.mean(x * x, axis=-1, keepdims=True) + RMS_EPS) * g


def _dot(a, b):
    return jnp.dot(a, b, preferred_element_type=jnp.float32)


def _dot_nt(a, b):
    return lax.dot_general(a, b, (((1,), (1,)), ((), ())),
                           preferred_element_type=jnp.float32)


def _log_sigmoid(x):
    return jnp.minimum(x, 0.0) - jnp.log1p(jnp.exp(-jnp.abs(x)))


def _sigmoid(x):
    return 1.0 / (1.0 + jnp.exp(-x))


def _memkv_kernel(mem_ref, g_ref, wkv_ref, k_ref, v_ref):
    hm = _rmsnorm(mem_ref[0], g_ref[...]).astype(jnp.bfloat16)
    kv = _dot(hm, wkv_ref[...])
    k_ref[0] = kv[:, :D_MODEL].astype(jnp.bfloat16)
    v_ref[0] = kv[:, D_MODEL:].astype(jnp.bfloat16)


def _band_bias(slopes, first):
    W = WINDOW
    qi = lax.broadcasted_iota(jnp.int32, (W, 2 * W), 0)
    kj = lax.broadcasted_iota(jnp.int32, (W, 2 * W), 1)
    dist = qi + W - kj
    valid = (dist >= 0) & (dist < W)
    if first:
        valid = valid & (kj >= W)
    distf = dist.astype(jnp.float32)
    return [jnp.where(valid, -(sl * distf), NEG_BIG) for sl in slopes]


def _cumsum_rows(tril, v, lane):
    n = MLSTM_HEADS
    hi = v.astype(jnp.bfloat16)
    r1 = v - hi.astype(jnp.float32)
    mid = r1.astype(jnp.bfloat16)
    lo = (r1 - mid.astype(jnp.float32)).astype(jnp.bfloat16)
    packed = jnp.where(lane < n, hi.astype(jnp.float32),
                       jnp.where(lane < 2 * n, pltpu.roll(mid.astype(jnp.float32), n, axis=1),
                                 jnp.where(lane < 3 * n, pltpu.roll(lo.astype(jnp.float32), 2 * n, axis=1),
                                           0.0)))
    r = _dot(tril, packed.astype(jnp.bfloat16))
    return r + pltpu.roll(r, 128 - n, axis=1) + pltpu.roll(r, 128 - 2 * n, axis=1)


def _alternate(*stage_iters):
    live = list(stage_iters)
    while live:
        live = [it for it in live if next(it, _DONE) is not _DONE]


_DONE = object()


def _mixer_kernel(sinks_ref, x_ref, g_ref, win_ref, wgate_ref, bgate_ref, convw_ref,
                  convb_ref, wout_ref, o_ref,
                  kv_buf_ref, qk_buf_ref, c_ref, m_ref, bias_ref, bias0_ref,
                  *step_refs, slopes):
    n = len(step_refs) // MIX_SEQS
    for bi in range(MIX_SEQS):
        _mixer_sequence(pl.program_id(1), sinks_ref, x_ref.at[bi], g_ref, win_ref, wgate_ref,
                        bgate_ref, convw_ref, convb_ref, wout_ref, o_ref.at[bi],
                        kv_buf_ref.at[bi], qk_buf_ref.at[bi], c_ref.at[bi], m_ref.at[bi],
                        bias_ref, bias0_ref, step_refs[bi * n:(bi + 1) * n], slopes)


def _mixer_sequence(s_idx, sinks_ref, x_ref, g_ref, win_ref, wgate_ref, bgate_ref, convw_ref,
                    convb_ref, wout_ref, o_ref,
                    kv_buf_ref, qk_buf_ref, c_ref, m_ref, bias_ref, bias0_ref,
                    step_refs, slopes):
    ts = MIX_TILE
    W = WINDOW
    L = MIX_ROWS
    bf16 = jnp.bfloat16

    @pl.when(s_idx == 0)
    def _():
        kv_buf_ref[:, 0:W, :] = jnp.zeros((2 * SWA_TILES, W, 128), bf16)
        qk_buf_ref[0:CARRY_ROWS, :] = jnp.zeros((CARRY_ROWS, 2 * MLSTM_WIDTH), jnp.float32)
        c_ref[...] = jnp.zeros_like(c_ref)
        m_ref[...] = jnp.zeros_like(m_ref)
        for head, (bias, bias0) in enumerate(zip(_band_bias(slopes, False), _band_bias(slopes, True))):
            bias_ref[head] = bias
            bias0_ref[head] = bias0

    @pl.when(s_idx == 1)
    def _():
        bias0_ref[...] = bias_ref[...]

    @pl.when(s_idx > 0)
    def _():
        kv_buf_ref[:, 0:W, :] = kv_buf_ref[:, ts:ts + W, :]
        qk_buf_ref[0:CARRY_ROWS, :] = qk_buf_ref[ts:ts + CARRY_ROWS, :]

    lane = lax.broadcasted_iota(jnp.int32, (L, 128), 1)
    low = lane < ATT_HEAD_DIM
    zero_tile = jnp.zeros((L, 128), bf16)
    lane_rc = lax.broadcasted_iota(jnp.int32, (SWA_ROWS, 128), 1)
    row = lax.broadcasted_iota(jnp.int32, (L, L), 0)
    col = lax.broadcasted_iota(jnp.int32, (L, L), 1)
    causal = row >= col
    tril = causal.astype(bf16)
    ones_col = (lane == 0).astype(bf16)

    def project(blk):
        r0 = blk * L
        q_ref, mqk_ref, vb_ref, og_ref, gates_ref = step_refs[5 * blk:5 * blk + 5]
        x = x_ref[r0:r0 + L, :]
        h = _rmsnorm(x, g_ref[...]).astype(bf16)

        def cols(c0, width):
            return _dot(h, win_ref[:, c0:c0 + width])

        for c0 in range(0, 2 * MLSTM_WIDTH, PROJ_COLS):
            qk_buf_ref[CARRY_ROWS + r0:CARRY_ROWS + r0 + L, c0:c0 + PROJ_COLS] = cols(O_MQK + c0, PROJ_COLS)
            conv = convb_ref[:, c0:c0 + PROJ_COLS]
            for j in range(QK_CONV_WIDTH):
                start = CARRY_ROWS + r0 - (QK_CONV_WIDTH - 1 - j)
                conv = conv + (convw_ref[j:j + 1, c0:c0 + PROJ_COLS]
                               * qk_buf_ref[start:start + L, c0:c0 + PROJ_COLS])
            mqk_ref[:, c0:c0 + PROJ_COLS] = conv * _sigmoid(conv)
            yield
        kv = cols(O_AK, 2 * ATT_KV_WIDTH)
        for which in range(2):
            t32 = kv[:, which * ATT_KV_WIDTH:(which + 1) * ATT_KV_WIDTH]
            tb = t32.astype(bf16)
            rb = pltpu.roll(t32, ATT_HEAD_DIM, axis=1).astype(bf16)
            base = which * SWA_TILES
            kv_buf_ref[base + 0, W + r0:W + r0 + L, :] = jnp.where(low, tb, zero_tile)
            kv_buf_ref[base + 1, W + r0:W + r0 + L, :] = jnp.where(low, zero_tile, rb)
            kv_buf_ref[base + 2, W + r0:W + r0 + L, :] = jnp.where(low, rb, zero_tile)
            kv_buf_ref[base + 3, W + r0:W + r0 + L, :] = jnp.where(low, zero_tile, tb)
        yield
        q_ref[...] = (cols(O_AQ, ATT_WIDTH) * (ATT_HEAD_DIM ** -0.5)).astype(bf16)
        yield
        vb_ref[...] = cols(O_MV, MLSTM_WIDTH).astype(bf16)
        yield
        og_ref[...] = _sigmoid(cols(O_MO, MLSTM_WIDTH))
        yield
        gates_ref[...] = _dot(h, wgate_ref[...]) + bgate_ref[...]

    def rest(blk):
        r0 = blk * L
        x = x_ref[r0:r0 + L, :]
        q_ref, mqk_ref, vb_ref, og_ref, gates_ref = step_refs[5 * blk:5 * blk + 5]

        bref = bias0_ref if blk == 0 else bias_ref
        out_tiles = []
        scores = []
        for hk in range(ATT_KV_HEADS):
            qq = jnp.concatenate([q_ref[:, hk * 256:hk * 256 + 128],
                                  q_ref[:, hk * 256 + 128:hk * 256 + 256]], axis=0)
            scores.append([_dot_nt(qq, kv_buf_ref[2 * hk + par, r0:r0 + 2 * W, :]) for par in range(2)])
        yield
        for hk in range(ATT_KV_HEADS):
            s_par = scores[hk]
            e_rows = [[], []]
            scale_rows = []
            for half in range(2):
                for rc in range(W // SWA_ROWS):
                    rr = rc * SWA_ROWS
                    rden = []
                    for par in range(2):
                        head = ATT_GROUP * hk + 2 * half + par
                        sc = (s_par[par][half * W + rr:half * W + rr + SWA_ROWS]
                              + bref[head, rr:rr + SWA_ROWS, :])
                        sink = sinks_ref[head]
                        mx = jnp.maximum(jnp.max(sc, axis=-1, keepdims=True), sink)
                        e = jnp.exp(sc - mx)
                        den = jnp.sum(e, axis=-1, keepdims=True) + jnp.exp(sink - mx)
                        e_rows[par].append(e.astype(bf16))
                        rden.append(1.0 / den)
                    scale_rows.append(jnp.where(lane_rc < ATT_HEAD_DIM, rden[0], rden[1]))
                if half == 0:
                    yield
            e_all = jnp.concatenate([jnp.concatenate(e_rows[0], axis=0),
                                     jnp.concatenate(e_rows[1], axis=0)], axis=1)
            vz = jnp.concatenate([kv_buf_ref[SWA_TILES + 2 * hk, r0:r0 + 2 * W, :],
                                  kv_buf_ref[SWA_TILES + 2 * hk + 1, r0:r0 + 2 * W, :]], axis=0)
            o = (_dot(e_all, vz) * jnp.concatenate(scale_rows, axis=0)).astype(bf16)
            out_tiles += [o[0:W], o[W:2 * W]]
            yield

        ig = gates_ref[:, 0:128]
        lf = _log_sigmoid(gates_ref[:, 128:256])
        b = _cumsum_rows(tril, lf, lane)
        a = ig - b
        a_t = a.T
        m_prev = m_ref[0:1, :]
        dmats = []
        rowmax = None
        for hd in range(MLSTM_HEADS):
            dm = jnp.where(causal, b[:, hd:hd + 1] + a_t[hd:hd + 1, :], NEG_BIG)
            dmats.append(dm)
            rm = jnp.max(dm, axis=-1, keepdims=True)
            rowmax = jnp.broadcast_to(rm, (L, 128)) if hd == 0 else jnp.where(lane == hd, rm, rowmax)
        inter = b + m_prev
        m_t = jnp.maximum(inter, rowmax)
        w_inter = jnp.exp(inter - m_t)
        e_negm = jnp.exp(-m_t)
        b_end = b[L - 1:L, :]
        g = b_end + a
        m_new = jnp.maximum(b_end + m_prev, jnp.max(g, axis=0, keepdims=True))
        decay = jnp.exp(b_end + m_prev - m_new)
        ws = jnp.exp(g - m_new)
        m_ref[0:1, :] = m_new
        yield
        for hd in range(MLSTM_HEADS):
            c0 = hd * MLSTM_HEAD_DIM
            dexp = jnp.exp(dmats[hd] - m_t[:, hd:hd + 1])
            qf = mqk_ref[:, c0:c0 + MLSTM_HEAD_DIM] * (MLSTM_HEAD_DIM ** -0.5)
            kf = mqk_ref[:, MLSTM_WIDTH + c0:MLSTM_WIDTH + c0 + MLSTM_HEAD_DIM]
            qb = qf.astype(bf16)
            kb = kf.astype(bf16)
            vb = vb_ref[:, c0:c0 + MLSTM_HEAD_DIM]
            v_ext = jnp.concatenate([vb, ones_col], axis=1)
            sc = _dot_nt(qb, kb) * dexp
            c_old = c_ref[hd]
            qw = (qf * w_inter[:, hd:hd + 1]).astype(bf16)
            num_ext = _dot(jnp.concatenate([qw, sc.astype(bf16)], axis=1),
                           jnp.concatenate([c_old.astype(bf16), v_ext], axis=0))
            num = num_ext[:, :MLSTM_HEAD_DIM]
            den = num_ext[:, MLSTM_HEAD_DIM:MLSTM_HEAD_DIM + 1]
            hh = num * (1.0 / jnp.maximum(jnp.abs(den), e_negm[:, hd:hd + 1]))
            og = og_ref[:, c0:c0 + MLSTM_HEAD_DIM]
            out_tiles.append((og * hh).astype(bf16))
            kwt = (kf * ws[:, hd:hd + 1]).T.astype(bf16)
            c_ref[hd] = decay[0:1, hd:hd + 1] * c_old + _dot(kwt, v_ext)
            yield

        cat = jnp.concatenate(out_tiles, axis=1)
        for c0 in range(0, D_MODEL, OUT_COLS):
            o_ref[r0:r0 + L, c0:c0 + OUT_COLS] = x[:, c0:c0 + OUT_COLS] + _dot(cat, wout_ref[:, c0:c0 + OUT_COLS])
            yield

    n_blk = ts // L
    _alternate(project(0))
    for blk in range(n_blk):
        _alternate(rest(blk), project(blk + 1) if blk + 1 < n_blk else iter(()))


def _ffn_kernel(x_ref, k_ref, v_ref, gx_ref, wq_ref, wo_ref, gf_ref, wup_ref, convw_ref,
                convb_ref, wdown_ref, gfin_ref, o_ref, cat_ref, u_buf_ref, act_ref):
    s_idx = pl.program_id(1)
    ts = SEQ_TILE

    @pl.when(s_idx == 0)
    def _():
        u_buf_ref[0:CARRY_ROWS, :] = jnp.zeros((CARRY_ROWS, D_FF), jnp.float32)

    @pl.when(s_idx > 0)
    def _():
        u_buf_ref[0:CARRY_ROWS, :] = u_buf_ref[ts:ts + CARRY_ROWS, :]

    x1 = x_ref[0]
    hq = _rmsnorm(x1, gx_ref[...]).astype(jnp.bfloat16)
    q = (_dot(hq, wq_ref[...]) * (X_HEAD_DIM ** -0.5)).astype(jnp.bfloat16)
    for hd in range(X_HEADS):
        c0 = hd * X_HEAD_DIM
        sc = _dot_nt(q[:, c0:c0 + X_HEAD_DIM], k_ref[0, :, c0:c0 + X_HEAD_DIM])
        e = jnp.exp(sc - jnp.max(sc, axis=-1, keepdims=True))
        den = jnp.sum(e, axis=-1, keepdims=True)
        o = _dot(e.astype(jnp.bfloat16), v_ref[0, :, c0:c0 + X_HEAD_DIM]) / den
        cat_ref[:, c0:c0 + X_HEAD_DIM] = o.astype(jnp.bfloat16)
    x2 = x1 + _dot(cat_ref[...], wo_ref[...])

    hf = _rmsnorm(x2, gf_ref[...]).astype(jnp.bfloat16)
    for j in range(D_FF // FFN_COLS):
        c0 = j * FFN_COLS
        u_buf_ref[CARRY_ROWS:CARRY_ROWS + ts, c0:c0 + FFN_COLS] = _dot(hf, wup_ref[:, c0:c0 + FFN_COLS])
        up = _dot(hf, wup_ref[:, D_FF + c0:D_FF + c0 + FFN_COLS])
        g = convb_ref[:, c0:c0 + FFN_COLS]
        for t in range(FFN_CONV_WIDTH):
            shift = FFN_CONV_WIDTH - 1 - t
            g = g + (convw_ref[t:t + 1, c0:c0 + FFN_COLS]
                     * u_buf_ref[CARRY_ROWS - shift:CARRY_ROWS - shift + ts, c0:c0 + FFN_COLS])
        act_ref[:, c0:c0 + FFN_COLS] = (g * _sigmoid(g) * up).astype(jnp.bfloat16)
    x3 = x2 + _dot(act_ref[...], wdown_ref[...])
    o_ref[0] = _rmsnorm(x3, gfin_ref[...])


def _const_spec(shape):
    return pl.BlockSpec(shape, lambda *_: (0,) * len(shape), pipeline_mode=pl.Buffered(1))


def _row(a):
    return a.reshape(1, -1).astype(jnp.float32)


def _tile_spec():
    return pl.BlockSpec((1, SEQ_TILE, D_MODEL), lambda b, s: (b, s, 0))


def _seq_params(n_inputs, fused_inputs):
    return pltpu.CompilerParams(dimension_semantics=("arbitrary", "arbitrary"),
                                vmem_limit_bytes=VMEM_LIMIT,
                                allow_input_fusion=[i in fused_inputs for i in range(n_inputs)])


def _mem_kv_call(mem, norm_mem_g, wkv):
    B = mem.shape[0]
    bf16 = jnp.bfloat16
    return pl.pallas_call(
        _memkv_kernel,
        grid=(B,),
        in_specs=[pl.BlockSpec((1, MEM_LEN, D_MODEL), lambda b: (b, 0, 0)),
                  _const_spec((1, D_MODEL)), _const_spec((D_MODEL, 2 * D_MODEL))],
        out_specs=[pl.BlockSpec((1, MEM_LEN, D_MODEL), lambda b: (b, 0, 0))] * 2,
        out_shape=[jax.ShapeDtypeStruct((B, MEM_LEN, D_MODEL), bf16)] * 2,
        compiler_params=pltpu.CompilerParams(dimension_semantics=("arbitrary",),
                                             vmem_limit_bytes=VMEM_LIMIT),
        name="mem_kv",
    )(mem, _row(norm_mem_g), wkv.astype(bf16))


def _mixer_call(x, norm_g, w_in, b_gate_if, conv_w, conv_b, sinks, w_out):
    B, S, D = x.shape
    bf16 = jnp.bfloat16
    f32 = jnp.float32
    w_in_main = w_in.astype(bf16)
    H = MLSTM_HEADS
    spread = lambda a: jnp.pad(a.reshape(a.shape[0], 2, H), ((0, 0), (0, 0), (0, 128 - H))).reshape(a.shape[0], GATE_PAD)
    w_gate = spread(w_in[:, IN_MAIN:IN_MAIN + 2 * H]).astype(bf16)
    b_gate = spread(b_gate_if.astype(f32).reshape(1, 2 * H))
    slopes = tuple(float(2.0 ** (-8.0 * (i + 1) / ATT_HEADS)) for i in range(ATT_HEADS))
    mix_spec = pl.BlockSpec((MIX_SEQS, MIX_TILE, D), lambda b, s: (b, s, 0))
    return pl.pallas_call(
        functools.partial(_mixer_kernel, slopes=slopes),
        grid=(B // MIX_SEQS, S // MIX_TILE),
        in_specs=[pl.BlockSpec(memory_space=pltpu.SMEM),
                  mix_spec,
                  _const_spec((1, D)),
                  _const_spec(w_in.shape),
                  _const_spec((D, GATE_PAD)),
                  _const_spec((1, GATE_PAD)),
                  _const_spec((QK_CONV_WIDTH, 2 * MLSTM_WIDTH)),
                  _const_spec((1, 2 * MLSTM_WIDTH)),
                  _const_spec((D, D))],
        out_specs=mix_spec,
        out_shape=jax.ShapeDtypeStruct((B, S, D), f32),
        scratch_shapes=[
            pltpu.VMEM((MIX_SEQS, 2 * SWA_TILES, WINDOW + MIX_TILE, 128), bf16),
            pltpu.VMEM((MIX_SEQS, CARRY_ROWS + MIX_TILE, 2 * MLSTM_WIDTH), f32),
            pltpu.VMEM((MIX_SEQS, MLSTM_HEADS, MLSTM_HEAD_DIM, 2 * MLSTM_HEAD_DIM), f32),
            pltpu.VMEM((MIX_SEQS, CARRY_ROWS, 128), f32),
            pltpu.VMEM((ATT_HEADS, WINDOW, 2 * WINDOW), f32),
            pltpu.VMEM((ATT_HEADS, WINDOW, 2 * WINDOW), f32),
        ] + [
            pltpu.VMEM((MIX_ROWS, ATT_WIDTH), bf16),
            pltpu.VMEM((MIX_ROWS, 2 * MLSTM_WIDTH), f32),
            pltpu.VMEM((MIX_ROWS, MLSTM_WIDTH), bf16),
            pltpu.VMEM((MIX_ROWS, MLSTM_WIDTH), f32),
            pltpu.VMEM((MIX_ROWS, GATE_PAD), f32),
        ] * (MIX_SEQS * MIX_TILE // MIX_ROWS),
        compiler_params=_seq_params(9, fused_inputs=(3, 8)),
        name="token_mixer",
    )(sinks.astype(f32), x, _row(norm_g), w_in_main, w_gate, b_gate,
      conv_w.astype(f32), _row(conv_b), w_out.astype(bf16))


def _ffn_call(x1, mem_k, mem_v, norm_x_g, wq, wo, norm_f_g, w_up, conv_w, conv_b, w_down,
              norm_final_g):
    B, S, D = x1.shape
    bf16 = jnp.bfloat16
    f32 = jnp.float32
    kv_spec = pl.BlockSpec((1, MEM_LEN, D), lambda b, s: (b, 0, 0))
    return pl.pallas_call(
        _ffn_kernel,
        grid=(B, S // SEQ_TILE),
        in_specs=[_tile_spec(), kv_spec, kv_spec,
                  _const_spec((1, D)), _const_spec((D, D)), _const_spec((D, D)),
                  _const_spec((1, D)), _const_spec((D, 2 * D_FF)),
                  _const_spec((FFN_CONV_WIDTH, D_FF)), _const_spec((1, D_FF)),
                  _const_spec((D_FF, D)), _const_spec((1, D))],
        out_specs=_tile_spec(),
        out_shape=jax.ShapeDtypeStruct((B, S, D), f32),
        scratch_shapes=[
            pltpu.VMEM((SEQ_TILE, D), bf16),
            pltpu.VMEM((CARRY_ROWS + SEQ_TILE, D_FF), f32),
            pltpu.VMEM((SEQ_TILE, D_FF), bf16),
        ],
        compiler_params=_seq_params(12, fused_inputs=(4, 5, 7, 10)),
        name="xattn_ffn",
    )(x1, mem_k, mem_v, _row(norm_x_g), wq.astype(bf16), wo.astype(bf16), _row(norm_f_g),
      w_up.astype(bf16), conv_w.astype(f32), _row(conv_b), w_down.astype(bf16),
      _row(norm_final_g))


def kernel(x, mem, norm_mix_g, w_in, b_gate_if, conv_qk_w, conv_qk_b, attn_sinks, w_out,
           norm_xattn_g, norm_mem_g, wq_x, wkv_x, wo_x, norm_ffn_g, w_up, conv_ffn_w,
           conv_ffn_b, w_down, norm_final_g):
    assert x.shape[2] == D_MODEL and x.shape[1] % SEQ_TILE == 0 and x.shape[1] % MIX_TILE == 0 and w_in.shape[0] == 1
    mem_k, mem_v = _mem_kv_call(mem, norm_mem_g[0], wkv_x[0])
    x1 = _mixer_call(x, norm_mix_g[0], w_in[0], b_gate_if[0], conv_qk_w[0], conv_qk_b[0],
                     attn_sinks[0], w_out[0])
    return _ffn_call(x1, mem_k, mem_v, norm_xattn_g[0], wq_x[0], wo_x[0], norm_ffn_g[0],
                     w_up[0], conv_ffn_w[0], conv_ffn_b[0], w_down[0], norm_final_g)
```
